```python
import math
import jax, jax.numpy as jnp
from jax import lax
import numpy as np

D_MODEL = 1024
BATCH = 8
SEQ = 2048
DEPTH = 4

MEM_LEN = 256
N_BRANCH = 4
BRANCH_WIDTH = D_MODEL // 4
RW_HEADS = 4
RW_HEAD_DIM = BRANCH_WIDTH // RW_HEADS
RW_DECAY_RANK = 64
RW_AAA_RANK = 64
RW_GATE_RANK = 160
RW_LN_EPS = 64e-5
RW_SHIFT_COLS = 3 * BRANCH_WIDTH + RW_DECAY_RANK + RW_AAA_RANK + RW_GATE_RANK
CONV_CH = BRANCH_WIDTH
CONV_WIDTH = 31
GLA_HEADS = 4
GLA_DK = BRANCH_WIDTH // 2 // GLA_HEADS
GLA_DV = BRANCH_WIDTH // GLA_HEADS
GLA_GATE_RANK = 16
GLA_TAU = 16.0
GLA_CHUNK = 64
GLA_EPS = 1e-5
FOX_HEADS = 4
FOX_HEAD_DIM = BRANCH_WIDTH // FOX_HEADS
FOX_BLOCK = 128
XA_HEADS = 4
XA_HEAD_DIM = D_MODEL // XA_HEADS
D_FF = ((8 * D_MODEL + 3 * 256 - 1) // (3 * 256)) * 256
ALPHA = (2.0 * DEPTH) ** 0.25
BETA = (8.0 * DEPTH) ** -0.25
LN_EPS = 1e-5

COL_SIZES = (
    RW_SHIFT_COLS,
    2 * CONV_CH,
    GLA_HEADS * GLA_DK,
    GLA_HEADS * GLA_DK,
    GLA_HEADS * GLA_DV,
    GLA_HEADS * GLA_DV,
    GLA_GATE_RANK,
    FOX_HEADS * FOX_HEAD_DIM,
    FOX_HEADS * FOX_HEAD_DIM,
    FOX_HEADS * FOX_HEAD_DIM,
    FOX_HEADS,
    N_BRANCH * D_MODEL,
)
D_IN = sum(COL_SIZES)

kernel_name = "hybrid_rwkv7_conv_gla_fox_deepnorm"


def _split(a, sizes):
    out, start = [], 0
    for s in sizes:
        out.append(a[..., start:start + s])
        start += s
    return out


def _layer_norm(x, g, b, eps=LN_EPS):
    xf = x.astype(jnp.float32)
    mu = jnp.mean(xf, axis=-1, keepdims=True)
    var = jnp.mean(jnp.square(xf - mu), axis=-1, keepdims=True)
    return ((xf - mu) * lax.rsqrt(var + eps)).astype(x.dtype) * g + b


def _token_shift(p, mu):
    prev = jnp.pad(p, ((0, 0), (1, 0), (0, 0)))[:, :-1]
    return p + (prev - p) * mu


def _rwkv7_scan(r, decay, k, v, a, b):
    Bsz, T, H, N = r.shape

    def step(S, inp):
        r_t, w_t, k_t, v_t, a_t, b_t = inp
        sa = jnp.einsum('bhij,bhj->bhi', S, a_t)
        S = (S * w_t[:, :, None, :] + sa[..., None] * b_t[:, :, None, :]
             + v_t[..., None] * k_t[:, :, None, :])
        return S, jnp.einsum('bhij,bhj->bhi', S, r_t)

    xs = tuple(jnp.moveaxis(t, 1, 0) for t in (r, decay, k, v, a, b))
    S0 = jnp.zeros((Bsz, H, N, N), jnp.float32)
    _, y = lax.scan(step, S0, xs)
    return jnp.moveaxis(y, 0, 1)


def _rwkv7_branch(p_rw, mu, w0, w2, a0, a2, g2, k_k, k_a, r_k, ln_g, ln_b):
    Bsz, T, _ = p_rw.shape
    dt = p_rw.dtype
    p = _token_shift(p_rw, mu)
    r, k, v, xw, xa, xg = _split(p, (BRANCH_WIDTH, BRANCH_WIDTH, BRANCH_WIDTH,
                                     RW_DECAY_RANK, RW_AAA_RANK, RW_GATE_RANK))
    w_log = -jax.nn.softplus(-(w0 + jnp.tanh(xw) @ w2).astype(jnp.float32)) - 0.5
    decay = jnp.exp(-jnp.exp(w_log))
    a = jax.nn.sigmoid((a0 + xa @ a2).astype(jnp.float32))
    g = jax.nn.sigmoid(xg) @ g2
    hs = lambda t: t.astype(jnp.float32).reshape(Bsz, T, RW_HEADS, RW_HEAD_DIM)
    kk = hs(k * k_k)
    kk = kk / jnp.maximum(jnp.sqrt(jnp.sum(kk * kk, axis=-1, keepdims=True)), 1e-12)
    k_mod = k.astype(jnp.float32) * (1.0 + (a - 1.0) * k_a.astype(jnp.float32))
    rh, kh, vh, ah, dh = hs(r), hs(k_mod), hs(v), hs(a), hs(decay)
    y = _rwkv7_scan(rh, dh, kh, vh, -kk, kk * ah)
    mu_y = jnp.mean(y, axis=-1, keepdims=True)
    var_y = jnp.mean(jnp.square(y - mu_y), axis=-1, keepdims=True)
    y = (y - mu_y) * lax.rsqrt(var_y + RW_LN_EPS)
    y = y.reshape(Bsz, T, BRANCH_WIDTH) * ln_g.astype(jnp.float32) + ln_b.astype(jnp.float32)
    bonus = jnp.sum(rh * kh * r_k.astype(jnp.float32), axis=-1, keepdims=True) * vh
    y = y + bonus.reshape(Bsz, T, BRANCH_WIDTH)
    return y.astype(dt) * g


def _conv_branch(p_cv, cv_w, cv_b, ln_g, ln_b):
    a, b = _split(p_cv, (CONV_CH, CONV_CH))
    u = a * jax.nn.sigmoid(b)
    u = lax.conv_general_dilated(
        u, cv_w[:, None, :], window_strides=(1,), padding=((CONV_WIDTH - 1, 0),),
        dimension_numbers=('NWC', 'WIO', 'NWC'), feature_group_count=CONV_CH) + cv_b
    u = _layer_norm(u, ln_g, ln_b)
    return jax.nn.silu(u)


def _gla_chunked(q, k, v, log_a):
    Bsz, T, H, dk = q.shape
    dv = v.shape[-1]
    L = GLA_CHUNK
    n = T // L
    to_chunks = lambda t: t.reshape(Bsz, n, L, H, t.shape[-1]).transpose(1, 0, 3, 2, 4)
    q, k, v, log_a = (to_chunks(t) for t in (q, k, v, log_a))
    b = jnp.cumsum(log_a, axis=-2)
    b_last = b[..., -1:, :]
    q_dec = q * jnp.exp(b)
    k_inv = k * jnp.exp(-b)
    k_end = k * jnp.exp(b_last - b)
    mask = jnp.tril(jnp.ones((L, L), dtype=bool))
    scores = jnp.where(mask, jnp.einsum('nbhtd,nbhsd->nbhts', q_dec, k_inv), 0.0)
    o_intra = jnp.einsum('nbhts,nbhsv->nbhtv', scores, v)

    def step(S, inp):
        qd, ke, vc, bl = inp
        o = jnp.einsum('bhtd,bhdv->bhtv', qd, S)
        S = S * jnp.exp(bl[:, :, 0, :])[..., None] + jnp.einsum('bhsd,bhsv->bhdv', ke, vc)
        return S, o

    S0 = jnp.zeros((Bsz, H, dk, dv), jnp.float32)
    _, o_inter = lax.scan(step, S0, (q_dec, k_end, v, b_last))
    o = o_intra + o_inter
    return o.transpose(1, 0, 3, 2, 4).reshape(Bsz, T, H, dv)


def _gla_branch(g_q, g_k, g_v, g_r, g_z, a2, ab, ln_g):
    Bsz, T, _ = g_q.shape
    dt = g_q.dtype
    log_a = jax.nn.log_sigmoid((g_z @ a2 + ab).astype(jnp.float32)) / GLA_TAU
    q = g_q.astype(jnp.float32).reshape(Bsz, T, GLA_HEADS, GLA_DK) * (GLA_DK ** -0.5)
    k = g_k.astype(jnp.float32).reshape(Bsz, T, GLA_HEADS, GLA_DK)
    v = g_v.astype(jnp.float32).reshape(Bsz, T, GLA_HEADS, GLA_DV)
    o = _gla_chunked(q, k, v, log_a.reshape(Bsz, T, GLA_HEADS, GLA_DK))
    o = o * lax.rsqrt(jnp.mean(o * o, axis=-1, keepdims=True) + GLA_EPS)
    o = o.reshape(Bsz, T, GLA_HEADS * GLA_DV).astype(dt) * ln_g
    return o * jax.nn.silu(g_r)


def _fox_branch(f_q, f_k, f_v, f_z, bf):
    Bsz, T, _ = f_q.shape
    heads = lambda t: t.reshape(Bsz, T, FOX_HEADS, FOX_HEAD_DIM).transpose(0, 2, 1, 3)
    qh = heads(f_q) * (FOX_HEAD_DIM ** -0.5)
    kh, vh = heads(f_k), heads(f_v)
    log_f = jax.nn.log_sigmoid((f_z + bf).astype(jnp.float32))
    c = jnp.cumsum(log_f, axis=1).transpose(0, 2, 1)
    diag = jnp.tril(jnp.ones((FOX_BLOCK, FOX_BLOCK), dtype=bool))
    outs = []
    for i in range(T // FOX_BLOCK):
        q0, q1 = i * FOX_BLOCK, (i + 1) * FOX_BLOCK
        logits = jnp.einsum('bhtd,bhsd->bhts', qh[:, :, q0:q1], kh[:, :, :q1]).astype(jnp.float32)
        logits = logits + c[:, :, q0:q1, None] - c[:, :, None, :q1]
        mask = jnp.concatenate([jnp.ones((FOX_BLOCK, q0), dtype=bool), diag], axis=1)
        probs = jax.nn.softmax(jnp.where(mask, logits, -jnp.inf), axis=-1).astype(vh.dtype)
        outs.append(jnp.einsum('bhts,bhsd->bhtd', probs, vh[:, :, :q1]))
    o = jnp.concatenate(outs, axis=2)
    return o.transpose(0, 2, 1, 3).reshape(Bsz, T, FOX_HEADS * FOX_HEAD_DIM)


def _cross_attention(x, mem, wq, wk, wv, wo):
    Bsz, T, D = x.shape
    M = mem.shape[1]
    q = (x @ wq).reshape(Bsz, T, XA_HEADS, XA_HEAD_DIM)
    k = (mem @ wk).reshape(Bsz, M, XA_HEADS, XA_HEAD_DIM)
    v = (mem @ wv).reshape(Bsz, M, XA_HEADS, XA_HEAD_DIM)
    s = jnp.einsum('bthd,bmhd->bhtm', q, k).astype(jnp.float32) * (XA_HEAD_DIM ** -0.5)
    p = jax.nn.softmax(s, axis=-1).astype(x.dtype)
    o = jnp.einsum('bhtm,bmhd->bthd', p, v).reshape(Bsz, T, D)
    return o @ wo


def _swiglu(x, w1, w3, w2):
    return (jax.nn.silu(x @ w1) * (x @ w3)) @ w2


def setup_inputs(seed: int = 0) -> dict:
    key = jax.random.key(seed)
    ks = iter(jax.random.split(key, 48))
    L, D = DEPTH, D_MODEL
    nrm = lambda shape, scale: jax.random.normal(next(ks), shape, jnp.float32) * scale
    return {
        "x": nrm((BATCH, SEQ, D), 1.0),
        "mem": nrm((BATCH, MEM_LEN, D), 1.0),
        "w_in": nrm((L, D, D_IN), D ** -0.5),
        "rw_mu": jax.random.uniform(next(ks), (L, RW_SHIFT_COLS), jnp.float32),
        "rw_w0": -1.0 + nrm((L, BRANCH_WIDTH), 0.5),
        "rw_w2": nrm((L, RW_DECAY_RANK, BRANCH_WIDTH), RW_DECAY_RANK ** -0.5),
        "rw_a0": nrm((L, BRANCH_WIDTH), 0.1),
        "rw_a2": nrm((L, RW_AAA_RANK, BRANCH_WIDTH), RW_AAA_RANK ** -0.5),
        "rw_g2": nrm((L, RW_GATE_RANK, BRANCH_WIDTH), RW_GATE_RANK ** -0.5),
        "rw_kk": 1.0 + nrm((L, BRANCH_WIDTH), 0.1),
        "rw_ka": 1.0 + nrm((L, BRANCH_WIDTH), 0.1),
        "rw_rk": nrm((L, RW_HEADS, RW_HEAD_DIM), 0.1),
        "rw_ln_g": 1.0 + nrm((L, BRANCH_WIDTH), 0.02),
        "rw_ln_b": nrm((L, BRANCH_WIDTH), 0.02),
        "rw_up": nrm((L, BRANCH_WIDTH, D), BRANCH_WIDTH ** -0.5),
        "cv_w": nrm((L, CONV_WIDTH, CONV_CH), CONV_WIDTH ** -0.5),
        "cv_b": nrm((L, CONV_CH), 0.02),
        "cv_ln_g": 1.0 + nrm((L, CONV_CH), 0.02),
        "cv_ln_b": nrm((L, CONV_CH), 0.02),
        "cv_up": nrm((L, CONV_CH, D), CONV_CH ** -0.5),
        "gla_a2": nrm((L, GLA_GATE_RANK, GLA_HEADS * GLA_DK), GLA_GATE_RANK ** -0.5),
        "gla_ab": nrm((L, GLA_HEADS * GLA_DK), 0.1),
        "gla_ln_g": 1.0 + nrm((L, GLA_HEADS * GLA_DV), 0.02),
        "gla_up": nrm((L, GLA_HEADS * GLA_DV, D), (GLA_HEADS * GLA_DV) ** -0.5),
        "fox_bf": 3.0 + nrm((L, FOX_HEADS), 0.5),
        "fox_up": nrm((L, FOX_HEADS * FOX_HEAD_DIM, D), (FOX_HEADS * FOX_HEAD_DIM) ** -0.5),
        "gate_b": nrm((L, N_BRANCH, D), 0.1),
        "w_out": nrm((L, D, D), BETA * D ** -0.5),
        "xa_wq": nrm((L, D, D), D ** -0.5),
        "xa_wk": nrm((L, D, D), D ** -0.5),
        "xa_wv": nrm((L, D, D), D ** -0.5),
        "xa_wo": nrm((L, D, D), BETA * D ** -0.5),
        "ffn_w1": nrm((L, D, D_FF), D ** -0.5),
        "ffn_w3": nrm((L, D, D_FF), D ** -0.5),
        "ffn_w2": nrm((L, D_FF, D), BETA * D_FF ** -0.5),
        "ln_g": 1.0 + nrm((L, 3, D), 0.02),
        "ln_b": nrm((L, 3, D), 0.02),
    }


def reference(x, mem, w_in, rw_mu, rw_w0, rw_w2, rw_a0, rw_a2, rw_g2, rw_kk, rw_ka, rw_rk,
              rw_ln_g, rw_ln_b, rw_up, cv_w, cv_b, cv_ln_g, cv_ln_b, cv_up, gla_a2, gla_ab,
              gla_ln_g, gla_up, fox_bf, fox_up, gate_b, w_out, xa_wq, xa_wk, xa_wv, xa_wo,
              ffn_w1, ffn_w3, ffn_w2, ln_g, ln_b):
    Bsz, T, D = x.shape
    for l in range(DEPTH):
        p = x @ w_in[l]
        (p_rw, p_cv, g_q, g_k, g_v, g_r, g_z,
         f_q, f_k, f_v, f_z, p_gate) = _split(p, COL_SIZES)
        u_rw = _rwkv7_branch(p_rw, rw_mu[l], rw_w0[l], rw_w2[l], rw_a0[l], rw_a2[l], rw_g2[l],
                             rw_kk[l], rw_ka[l], rw_rk[l], rw_ln_g[l], rw_ln_b[l]) @ rw_up[l]
        u_cv = _conv_branch(p_cv, cv_w[l], cv_b[l], cv_ln_g[l], cv_ln_b[l]) @ cv_up[l]
        u_gla = _gla_branch(g_q, g_k, g_v, g_r, g_z, gla_a2[l], gla_ab[l], gla_ln_g[l]) @ gla_up[l]
        u_fox = _fox_branch(f_q, f_k, f_v, f_z, fox_bf[l]) @ fox_up[l]
        gates = jax.nn.sigmoid(p_gate.reshape(Bsz, T, N_BRANCH, D) + gate_b[l])
        merged = (gates[:, :, 0] * u_rw + gates[:, :, 1] * u_cv
                  + gates[:, :, 2] * u_gla + gates[:, :, 3] * u_fox)
        x = _layer_norm(ALPHA * x + merged @ w_out[l], ln_g[l, 0], ln_b[l, 0])
        y = _cross_attention(x, mem, xa_wq[l], xa_wk[l], xa_wv[l], xa_wo[l])
        x = _layer_norm(ALPHA * x + y, ln_g[l, 1], ln_b[l, 1])
        y = _swiglu(x, ffn_w1[l], ffn_w3[l], ffn_w2[l])
        x = _layer_norm(ALPHA * x + y, ln_g[l, 2], ln_b[l, 2])
    return x
```

```python
import functools

import jax
import jax.numpy as jnp
from jax import lax
from jax.experimental import pallas as pl
from jax.experimental.pallas import tpu as pltpu

F32 = jnp.float32
BF16 = jnp.bfloat16

D_MODEL = 1024
WIDTH = 256
HEADS = 4
HEAD_DIM = WIDTH // HEADS
CHUNK = 64
RW_COLS = 1056
RW_PAD = 1152
CV_COLS = 512
GLA_COLS = 784
GLA_PAD = 896
GLA_DK = 32
FOX_COLS = 772
FOX_PAD = 896
SMALL_COLS = RW_PAD + CV_COLS + GLA_PAD + FOX_PAD
GATE_COLS = 4 * D_MODEL
CONV_WIDTH = 31
CONV_HALO = 32
D_FF = 2816
FF_CHUNK = 1408
RW_LN_EPS = 64e-5
GLA_EPS = 1e-5
GLA_TAU = 16.0
LN_EPS = 1e-5
NEG_BIG = -1e30
VMEM_LIMIT = 56 * 1024 * 1024


def _bdot(a, b):
    return jnp.dot(a.astype(BF16), b.astype(BF16), preferred_element_type=F32)


def _bdot_nt(a, b):
    return lax.dot_general(a.astype(BF16), b.astype(BF16), (((1,), (1,)), ((), ())),
                           preferred_element_type=F32)


def _bdot_tn(a, b):
    return lax.dot_general(a.astype(BF16), b.astype(BF16), (((0,), (0,)), ((), ())),
                           preferred_element_type=F32)


def _split3(x):
    hi = x.astype(BF16)
    r1 = x - hi.astype(F32)
    mid = r1.astype(BF16)
    lo = (r1 - mid.astype(F32)).astype(BF16)
    return hi, mid, lo


def _dot3_right(x, m):
    return sum(jnp.dot(p, m, preferred_element_type=F32) for p in _split3(x))


def _dot3_left(m, x):
    return sum(jnp.dot(m, p, preferred_element_type=F32) for p in _split3(x))


def _softplus(z):
    return jnp.maximum(z, 0.0) + jnp.log1p(jnp.exp(-jnp.abs(z)))


def _log_sigmoid(z):
    return -_softplus(-z)


def _sigmoid(z):
    return jax.nn.sigmoid(z)


def _layer_norm(z, g, b):
    mu = jnp.mean(z, axis=-1, keepdims=True)
    d = z - mu
    var = jnp.mean(d * d, axis=-1, keepdims=True)
    return d * lax.rsqrt(var + LN_EPS) * g + b


def _head_masks(rows, row_group, cols, col_group):
    ri = lax.broadcasted_iota(jnp.int32, (rows, cols), 0)
    ci = lax.broadcasted_iota(jnp.int32, (rows, cols), 1)
    same = (ri // row_group) == (ci // col_group)
    return same, ri, ci


def _stack4(x):
    return jnp.concatenate([x, x, x, x], axis=0)


def _fold4(x, n):
    return x[0:n] + x[n:2 * n] + x[2 * n:3 * n] + x[3 * n:4 * n]


def _const_spec(shape):
    nd = len(shape)
    return pl.BlockSpec(shape, lambda *_: (0,) * nd)


def _params(sem):
    return pltpu.CompilerParams(dimension_semantics=sem, vmem_limit_bytes=VMEM_LIMIT)


def _mm_kernel(a_ref, w_ref, o_ref):
    o_ref[...] = _bdot(a_ref[...], w_ref[...]).astype(o_ref.dtype)


def _matmul(a, w, out_dtype, tm, tn):
    m, k = a.shape
    n = w.shape[1]
    return pl.pallas_call(
        _mm_kernel,
        grid=(m // tm, n // tn),
        in_specs=[pl.BlockSpec((tm, k), lambda i, j: (i, 0)),
                  pl.BlockSpec((k, tn), lambda i, j: (0, j))],
        out_specs=pl.BlockSpec((tm, tn), lambda i, j: (i, j)),
        out_shape=jax.ShapeDtypeStruct((m, n), out_dtype),
        compiler_params=_params(("parallel", "parallel")),
        name="matmul",
    )(a, w)


def _proj_kernel(x_ref, w_ref, rw_ref, cv_ref, gla_ref, fox_ref):
    xb = x_ref[...].astype(BF16)
    off = 0
    for ref, n in ((rw_ref, RW_PAD), (cv_ref, CV_COLS), (gla_ref, GLA_PAD), (fox_ref, FOX_PAD)):
        ref[...] = jnp.dot(xb, w_ref[:, off:off + n], preferred_element_type=F32)
        off += n


def _in_proj(x2, w_small, tm):
    m = x2.shape[0]
    widths = (RW_PAD, CV_COLS, GLA_PAD, FOX_PAD)
    return pl.pallas_call(
        _proj_kernel,
        grid=(m // tm,),
        in_specs=[pl.BlockSpec((tm, D_MODEL), lambda i: (i, 0)),
                  _const_spec((D_MODEL, SMALL_COLS))],
        out_specs=[pl.BlockSpec((tm, n), lambda i: (i, 0)) for n in widths],
        out_shape=[jax.ShapeDtypeStruct((m, n), F32) for n in widths],
        compiler_params=_params(("parallel",)),
        name="in_proj",
    )(x2, w_small)


def _rwkv_kernel(p_ref, mu_ref, vec_ref, w2_ref, a2_ref, g2_ref, o_ref, st_ref, prev_ref, *, tb):
    c = WIDTH
    n = CHUNK

    @pl.when(pl.program_id(1) == 0)
    def _():
        st_ref[...] = jnp.zeros_like(st_ref)
        prev_ref[...] = jnp.zeros_like(prev_ref)

    p = p_ref[0]
    row = lax.broadcasted_iota(jnp.int32, (tb, 1), 0)
    prev = jnp.where(row == 0, prev_ref[...], pltpu.roll(p, 1, axis=0))
    prev_ref[...] = p[tb - 1:tb, :]
    ps = p + (prev - p) * mu_ref[...]

    w0, a0, k_k, k_a = vec_ref[0:1], vec_ref[1:2], vec_ref[2:3], vec_ref[3:4]
    r_k, ln_g, ln_b = vec_ref[4:5], vec_ref[5:6], vec_ref[6:7]

    r = ps[:, 0:c]
    k = ps[:, c:2 * c]
    v = ps[:, 2 * c:3 * c]
    wa = ps[:, 3 * c:3 * c + 128]
    xg = ps[:, 3 * c + 128:3 * c + 384]

    w_log = -_softplus(-(w0 + _bdot(jnp.tanh(wa), w2_ref[...]))) - 0.5
    logw = -jnp.exp(w_log)
    a_sig = _sigmoid(a0 + _bdot(wa, a2_ref[...]))
    g = _bdot(_sigmoid(xg), g2_ref[...])

    same, ri, ci = _head_masks(c, n, c, n)
    tr, tc = ri % n, ci % n
    m_strict = same & (tr > tc)
    m_incl = same & (tr >= tc)
    ones_bd = jnp.where(same, 1.0, 0.0).astype(BF16)
    tril_bd = jnp.where(m_incl, 1.0, 0.0).astype(BF16)
    eye = jnp.where(same & (tr == tc), 1.0, 0.0)

    kk = k * k_k
    kk = kk / jnp.maximum(jnp.sqrt(_dot3_right(kk * kk, ones_bd)), 1e-12)
    k_mod = k * (1.0 + (a_sig - 1.0) * k_a)
    a_vec = -kk
    b_vec = kk * a_sig

    bd = lambda x: jnp.where(same, _stack4(x), 0.0)

    ys = []
    for ch in range(tb // n):
        sl = slice(ch * n, (ch + 1) * n)
        lw = logw[sl]
        cum = _dot3_left(tril_bd[0:n, 0:n], lw)
        last = cum[n - 1:n, :]
        g_in = jnp.exp(cum)
        g_inv = jnp.exp(-cum)
        g_end = jnp.exp(last - cum)
        at = a_vec[sl] * jnp.exp(cum - lw)
        rt = r[sl] * g_in
        bt = b_vec[sl] * g_inv
        kt = k_mod[sl] * g_inv
        bh = b_vec[sl] * g_end
        kh = k_mod[sl] * g_end
        vc = v[sl]

        lhs = jnp.concatenate([bd(at), bd(rt)], axis=0).astype(BF16)
        rhs = jnp.concatenate([_stack4(bt), _stack4(kt)], axis=0).astype(BF16)
        sc = _bdot_nt(lhs, rhs)
        m_ab = jnp.where(m_strict, sc[0:c, 0:c], 0.0)
        m_ak = jnp.where(m_strict, sc[0:c, c:2 * c], 0.0)
        m_rb = jnp.where(m_incl, sc[c:2 * c, 0:c], 0.0)
        m_rk = jnp.where(m_incl, sc[c:2 * c, c:2 * c], 0.0)

        t_inv = eye + m_ab
        m_pow = m_ab
        for _ in range(5):
            m_pow = _bdot(m_pow, m_pow)
            t_inv = t_inv + _bdot(m_pow, t_inv)

        st = st_ref[...]
        ws = _bdot_nt(lhs, st)
        bdv = bd(vc)
        u_bd = _bdot(t_inv, ws[0:c] + _bdot(m_ak, bdv))
        y_bd = ws[c:2 * c] + _bdot(m_rb, u_bd) + _bdot(m_rk, bdv)
        ys.append(_fold4(y_bd, n))
        u = _fold4(u_bd, n)
        upd = _bdot_tn(jnp.concatenate([u, vc], axis=0), jnp.concatenate([bh, kh], axis=0))
        st_ref[...] = st * jnp.exp(last) + jnp.where(same, upd, 0.0)

    y = jnp.concatenate(ys, axis=0)
    inv_n = 1.0 / HEAD_DIM
    mu_y = _dot3_right(y, ones_bd) * inv_n
    dy = y - mu_y
    var_y = _dot3_right(dy * dy, ones_bd) * inv_n
    yn = dy * lax.rsqrt(var_y + RW_LN_EPS) * ln_g + ln_b
    bonus = _dot3_right(r * k_mod * r_k, ones_bd) * v
    o_ref[0] = (yn + bonus) * g


def _rwkv(p_rw, mu, vec, w2, a2, g2, tb):
    bsz, t, _ = p_rw.shape
    return pl.pallas_call(
        functools.partial(_rwkv_kernel, tb=tb),
        grid=(bsz, t // tb),
        in_specs=[pl.BlockSpec((1, tb, RW_PAD), lambda b, i: (b, i, 0)),
                  _const_spec((1, RW_PAD)), _const_spec((8, WIDTH)),
                  _const_spec((128, WIDTH)), _const_spec((128, WIDTH)), _const_spec((WIDTH, WIDTH))],
        out_specs=pl.BlockSpec((1, tb, WIDTH), lambda b, i: (b, i, 0)),
        out_shape=jax.ShapeDtypeStruct((bsz, t, WIDTH), F32),
        scratch_shapes=[pltpu.VMEM((WIDTH, WIDTH), F32), pltpu.VMEM((1, RW_PAD), F32)],
        compiler_params=_params(("parallel", "arbitrary")),
        name="rwkv7",
    )(p_rw, mu, vec, w2, a2, g2)


def _conv_kernel(p_ref, w_ref, vec_ref, o_ref, buf_ref, *, tb):
    @pl.when(pl.program_id(1) == 0)
    def _():
        buf_ref[0:CONV_HALO, :] = jnp.zeros((CONV_HALO, WIDTH), F32)

    p = p_ref[0]
    buf_ref[CONV_HALO:CONV_HALO + tb, :] = p[:, 0:WIDTH] * _sigmoid(p[:, WIDTH:2 * WIDTH])
    acc = jnp.zeros((tb, WIDTH), F32) + vec_ref[0:1]
    first = CONV_HALO - (CONV_WIDTH - 1)
    for kx in range(CONV_WIDTH):
        acc = acc + buf_ref[first + kx:first + kx + tb, :] * w_ref[kx:kx + 1, :]
    buf_ref[0:CONV_HALO, :] = buf_ref[tb:tb + CONV_HALO, :]
    u = _layer_norm(acc, vec_ref[1:2], vec_ref[2:3])
    o_ref[0] = u * _sigmoid(u)


def _conv(p_cv, w, vec, tb):
    bsz, t, _ = p_cv.shape
    return pl.pallas_call(
        functools.partial(_conv_kernel, tb=tb),
        grid=(bsz, t // tb),
        in_specs=[pl.BlockSpec((1, tb, CV_COLS), lambda b, i: (b, i, 0)),
                  _const_spec((CONV_HALO, WIDTH)), _const_spec((8, WIDTH))],
        out_specs=pl.BlockSpec((1, tb, WIDTH), lambda b, i: (b, i, 0)),
        out_shape=jax.ShapeDtypeStruct((bsz, t, WIDTH), F32),
        scratch_shapes=[pltpu.VMEM((tb + CONV_HALO, WIDTH), F32)],
        compiler_params=_params(("parallel", "arbitrary")),
        name="conv",
    )(p_cv, w, vec)


def _gla_kernel(p_ref, a2_ref, vec_ref, lng_ref, o_ref, st_ref, *, tb):
    c = WIDTH
    n = CHUNK
    dk = HEADS * GLA_DK

    @pl.when(pl.program_id(1) == 0)
    def _():
        st_ref[...] = jnp.zeros_like(st_ref)

    p = p_ref[0]
    q = p[:, 0:dk] * (GLA_DK ** -0.5)
    k = p[:, dk:2 * dk]
    v = p[:, 2 * dk:2 * dk + c]
    gr = p[:, 2 * dk + c:2 * dk + 2 * c]
    z = p[:, 2 * dk + 2 * c:2 * dk + 2 * c + 128]
    log_a = _log_sigmoid(_bdot(z, a2_ref[...]) + vec_ref[0:1]) / GLA_TAU

    same_q, _, _ = _head_masks(c, n, dk, GLA_DK)
    same, ri, ci = _head_masks(c, n, c, n)
    m_incl = same & ((ri % n) >= (ci % n))
    tril = jnp.where(m_incl[0:n, 0:n], 1.0, 0.0).astype(BF16)
    ones_bd = jnp.where(same, 1.0, 0.0).astype(BF16)

    os_ = []
    for ch in range(tb // n):
        sl = slice(ch * n, (ch + 1) * n)
        cum = _dot3_left(tril, log_a[sl])
        last = cum[n - 1:n, :]
        q_dec = q[sl] * jnp.exp(cum)
        k_inv = k[sl] * jnp.exp(-cum)
        k_end = k[sl] * jnp.exp(last - cum)
        vc = v[sl]
        q_bd = jnp.where(same_q, _stack4(q_dec), 0.0).astype(BF16)
        sc = jnp.where(m_incl, _bdot_nt(q_bd, _stack4(k_inv)), 0.0)
        st = st_ref[...]
        o_bd = _bdot(sc, jnp.where(same, _stack4(vc), 0.0)) + _bdot_nt(q_bd, st)
        os_.append(_fold4(o_bd, n))
        st_ref[...] = st * jnp.exp(last) + jnp.where(same_q, _bdot_tn(vc, k_end), 0.0)

    o = jnp.concatenate(os_, axis=0)
    ms = _dot3_right(o * o, ones_bd) * (1.0 / HEAD_DIM)
    o = o * lax.rsqrt(ms + GLA_EPS) * lng_ref[...]
    o_ref[0] = o * (gr * _sigmoid(gr))


def _gla(p_gla, a2, vec, ln_g, tb):
    bsz, t, _ = p_gla.shape
    return pl.pallas_call(
        functools.partial(_gla_kernel, tb=tb),
        grid=(bsz, t // tb),
        in_specs=[pl.BlockSpec((1, tb, GLA_PAD), lambda b, i: (b, i, 0)),
                  _const_spec((128, 128)), _const_spec((8, 128)), _const_spec((1, WIDTH))],
        out_specs=pl.BlockSpec((1, tb, WIDTH), lambda b, i: (b, i, 0)),
        out_shape=jax.ShapeDtypeStruct((bsz, t, WIDTH), F32),
        scratch_shapes=[pltpu.VMEM((WIDTH, HEADS * GLA_DK), F32)],
        compiler_params=_params(("parallel", "arbitrary")),
        name="gla",
    )(p_gla, a2, vec, ln_g)


def _fox_gate_kernel(z_ref, bf_ref, c_ref):
    x = _log_sigmoid(z_ref[0] + bf_ref[...])
    t = x.shape[1]
    lane = lax.broadcasted_iota(jnp.int32, x.shape, 1)
    sh = 1
    while sh < t:
        x = x + jnp.where(lane >= sh, pltpu.roll(x, sh, axis=1), 0.0)
        sh *= 2
    c_ref[0] = x


def _fox_gate(z_rows, bf):
    bsz, hp, t = z_rows.shape
    return pl.pallas_call(
        _fox_gate_kernel,
        grid=(bsz,),
        in_specs=[pl.BlockSpec((1, hp, t), lambda b: (b, 0, 0)), _const_spec((hp, 1))],
        out_specs=pl.BlockSpec((1, hp, t), lambda b: (b, 0, 0)),
        out_shape=jax.ShapeDtypeStruct((bsz, hp, t), F32),
        compiler_params=_params(("parallel",)),
        name="fox_gate",
    )(z_rows, bf)


def _fox_kernel(q_ref, kv_ref, ccol_ref, crow_ref, o_ref, *, tq):
    c = WIDTH
    i = pl.program_id(1)
    q = q_ref[0, :, 0:c] * (HEAD_DIM ** -0.5)
    lane_head = lax.broadcasted_iota(jnp.int32, (1, c), 1) // HEAD_DIM
    row_g = i * tq + lax.broadcasted_iota(jnp.int32, (tq, 1), 0)
    col_l = lax.broadcasted_iota(jnp.int32, (1, tq), 1)
    out = jnp.zeros((tq, c), F32)
    for h in range(HEADS):
        hm = lane_head == h
        qm = jnp.where(hm, q, 0.0).astype(BF16)
        cq = ccol_ref[0, :, h:h + 1]

        def body(kb, carry, qm=qm, cq=cq, h=h):
            m, l, acc = carry
            ks = pl.multiple_of(kb * tq, tq)
            kblk = kv_ref[0, pl.ds(ks, tq), c:2 * c]
            vblk = kv_ref[0, pl.ds(ks, tq), 2 * c:3 * c]
            ck = crow_ref[0, h:h + 1, pl.ds(ks, tq)]
            s = _bdot_nt(qm, kblk) + cq - ck
            s = jnp.where(row_g >= ks + col_l, s, NEG_BIG)
            m_new = jnp.maximum(m, jnp.max(s, axis=1, keepdims=True))
            alpha = jnp.exp(m - m_new)
            pr = jnp.exp(s - m_new)
            l = alpha * l + jnp.sum(pr, axis=1, keepdims=True)
            acc = alpha * acc + _bdot(pr, vblk)
            return m_new, l, acc

        init = (jnp.full((tq, 1), NEG_BIG, F32), jnp.zeros((tq, 1), F32), jnp.zeros((tq, c), F32))
        _, l, acc = lax.fori_loop(0, i + 1, body, init)
        out = out + jnp.where(hm, acc / l, 0.0)
    o_ref[0] = out


def _fox(p_fox, c_col, c_row, tq):
    bsz, t, _ = p_fox.shape
    hp = c_row.shape[1]
    return pl.pallas_call(
        functools.partial(_fox_kernel, tq=tq),
        grid=(bsz, t // tq),
        in_specs=[pl.BlockSpec((1, tq, FOX_PAD), lambda b, i: (b, i, 0)),
                  pl.BlockSpec((1, t, FOX_PAD), lambda b, i: (b, 0, 0)),
                  pl.BlockSpec((1, tq, hp), lambda b, i: (b, i, 0)),
                  pl.BlockSpec((1, hp, t), lambda b, i: (b, 0, 0))],
        out_specs=pl.BlockSpec((1, tq, WIDTH), lambda b, i: (b, i, 0)),
        out_shape=jax.ShapeDtypeStruct((bsz, t, WIDTH), F32),
        compiler_params=_params(("parallel", "arbitrary")),
        name="fox",
    )(p_fox, p_fox, c_col, c_row)


def _merge_kernel(x_ref, y0_ref, y1_ref, y2_ref, y3_ref, wg_ref, gb_ref, up_ref, wo_ref, ln_ref,
                  o_ref, *, alpha):
    x = x_ref[...]
    xb = x.astype(BF16)
    merged = None
    for j, y_ref in enumerate((y0_ref, y1_ref, y2_ref, y3_ref)):
        gate = _sigmoid(jnp.dot(xb, wg_ref[:, j * D_MODEL:(j + 1) * D_MODEL],
                                preferred_element_type=F32) + gb_ref[j:j + 1, :])
        term = gate * _bdot(y_ref[...], up_ref[j])
        merged = term if merged is None else merged + term
    z = alpha * x + _bdot(merged, wo_ref[...])
    o_ref[...] = _layer_norm(z, ln_ref[0:1], ln_ref[1:2])


def _merge(x2, ys, wg, gb, ups, wo, ln, alpha, tm):
    m = x2.shape[0]
    row = lambda n: pl.BlockSpec((tm, n), lambda i: (i, 0))
    return pl.pallas_call(
        functools.partial(_merge_kernel, alpha=alpha),
        grid=(m // tm,),
        in_specs=[row(D_MODEL)] + [row(WIDTH)] * 4 + [
            _const_spec((D_MODEL, GATE_COLS)), _const_spec((4, D_MODEL)),
            _const_spec((4, WIDTH, D_MODEL)), _const_spec((D_MODEL, D_MODEL)),
            _const_spec((2, D_MODEL))],
        out_specs=row(D_MODEL),
        out_shape=jax.ShapeDtypeStruct((m, D_MODEL), F32),
        compiler_params=_params(("parallel",)),
        name="merge",
    )(x2, *ys, wg, gb, ups, wo, ln)


def _xattn_kernel(x_ref, k_ref, v_ref, wq_ref, wo_ref, ln_ref, o_ref, *, alpha):
    x = x_ref[0]
    q = _bdot(x, wq_ref[...])
    hd = D_MODEL // HEADS
    outs = []
    for h in range(HEADS):
        sl = slice(h * hd, (h + 1) * hd)
        s = _bdot_nt(q[:, sl], k_ref[0, :, sl]) * (hd ** -0.5)
        e = jnp.exp(s - jnp.max(s, axis=-1, keepdims=True))
        pr = e / jnp.sum(e, axis=-1, keepdims=True)
        outs.append(_bdot(pr, v_ref[0, :, sl]))
    y = _bdot(jnp.concatenate(outs, axis=1), wo_ref[...])
    o_ref[0] = _layer_norm(alpha * x + y, ln_ref[0:1], ln_ref[1:2])


def _xattn(x, kv, wq, wo, ln, alpha, tm):
    bsz, t, _ = x.shape
    mlen = kv.shape[1]
    return pl.pallas_call(
        functools.partial(_xattn_kernel, alpha=alpha),
        grid=(bsz, t // tm),
        in_specs=[pl.BlockSpec((1, tm, D_MODEL), lambda b, i: (b, i, 0)),
                  pl.BlockSpec((1, mlen, D_MODEL), lambda b, i: (b, 0, 0)),
                  pl.BlockSpec((1, mlen, D_MODEL), lambda b, i: (b, 0, 1)),
                  _const_spec((D_MODEL, D_MODEL)), _const_spec((D_MODEL, D_MODEL)),
                  _const_spec((2, D_MODEL))],
        out_specs=pl.BlockSpec((1, tm, D_MODEL), lambda b, i: (b, i, 0)),
        out_shape=jax.ShapeDtypeStruct((bsz, t, D_MODEL), F32),
        compiler_params=_params(("parallel", "parallel")),
        name="xattn",
    )(x, kv, kv, wq, wo, ln)


def _ffn_kernel(x_ref, w1_ref, w3_ref, w2_ref, ln_ref, o_ref, *, alpha):
    x = x_ref[...]
    xb = x.astype(BF16)
    y = None
    for c0 in range(0, D_FF, FF_CHUNK):
        h1 = jnp.dot(xb, w1_ref[:, c0:c0 + FF_CHUNK], preferred_element_type=F32)
        h3 = jnp.dot(xb, w3_ref[:, c0:c0 + FF_CHUNK], preferred_element_type=F32)
        part = _bdot(h1 * _sigmoid(h1) * h3, w2_ref[c0:c0 + FF_CHUNK, :])
        y = part if y is None else y + part
    o_ref[...] = _layer_norm(alpha * x + y, ln_ref[0:1], ln_ref[1:2])


def _ffn(x2, w1, w3, w2, ln, alpha, tm):
    m = x2.shape[0]
    return pl.pallas_call(
        functools.partial(_ffn_kernel, alpha=alpha),
        grid=(m // tm,),
        in_specs=[pl.BlockSpec((tm, D_MODEL), lambda i: (i, 0)),
                  _const_spec((D_MODEL, D_FF)), _const_spec((D_MODEL, D_FF)),
                  _const_spec((D_FF, D_MODEL)), _const_spec((2, D_MODEL))],
        out_specs=pl.BlockSpec((tm, D_MODEL), lambda i: (i, 0)),
        out_shape=jax.ShapeDtypeStruct((m, D_MODEL), F32),
        compiler_params=_params(("parallel",)),
        name="ffn",
    )(x2, w1, w3, w2, ln)


def _pad_cols(a, n):
    return jnp.pad(a, ((0, 0), (0, n - a.shape[1])))


def _pad_rows(a, n):
    return jnp.pad(a, ((0, n - a.shape[0]), (0, 0)))


def _rows8(*vs):
    a = jnp.stack([v.reshape(-1) for v in vs])
    return _pad_rows(a, 8)


def kernel(x, mem, w_in, rw_mu, rw_w0, rw_w2, rw_a0, rw_a2, rw_g2, rw_kk, rw_ka, rw_rk, rw_ln_g, rw_ln_b, rw_up, cv_w, cv_b, cv_ln_g, cv_ln_b, cv_up, gla_a2, gla_ab, gla_ln_g, gla_up, fox_bf, fox_up, gate_b, w_out, xa_wq, xa_wk, xa_wv, xa_wo, ffn_w1, ffn_w3, ffn_w2, ln_g, ln_b):
    bsz, t, d = x.shape
    depth = w_in.shape[0]
    alpha = (2.0 * depth) ** 0.25
    m = bsz * t
    mlen = mem.shape[1]
    tm = min(256, t)
    tb = min(256, t)
    mem2 = mem.reshape(bsz * mlen, d)

    for l in range(depth):
        wl = w_in[l]
        o1 = RW_COLS + CV_COLS
        o2 = o1 + GLA_COLS
        o3 = o2 + FOX_COLS
        w_small = jnp.concatenate([
            _pad_cols(wl[:, 0:RW_COLS], RW_PAD), wl[:, RW_COLS:o1],
            _pad_cols(wl[:, o1:o2], GLA_PAD), _pad_cols(wl[:, o2:o3], FOX_PAD)], axis=1).astype(BF16)
        w_gate = wl[:, o3:].astype(BF16)
        rw_vec = _rows8(rw_w0[l], rw_a0[l], rw_kk[l], rw_ka[l], rw_rk[l], rw_ln_g[l], rw_ln_b[l])
        rw_w2p = _pad_rows(rw_w2[l], 128).astype(BF16)
        rw_a2p = jnp.pad(rw_a2[l], ((64, 0), (0, 0))).astype(BF16)
        rw_g2p = _pad_rows(rw_g2[l], WIDTH).astype(BF16)
        mu = _pad_cols(rw_mu[l][None, :], RW_PAD)
        cv_vec = _rows8(cv_b[l], cv_ln_g[l], cv_ln_b[l])
        cv_wp = _pad_rows(cv_w[l], CONV_HALO)
        gla_a2p = _pad_rows(gla_a2[l], 128).astype(BF16)
        gla_vec = _rows8(gla_ab[l])
        ups = jnp.stack([rw_up[l], cv_up[l], gla_up[l], fox_up[l]]).astype(BF16)
        lns = [jnp.stack([ln_g[l, s], ln_b[l, s]]) for s in range(3)]

        x2 = x.reshape(m, d)
        p_rw, p_cv, p_gla, p_fox = _in_proj(x2, w_small, min(512, m))
        p_rw = p_rw.reshape(bsz, t, RW_PAD)
        p_cv = p_cv.reshape(bsz, t, CV_COLS)
        p_gla = p_gla.reshape(bsz, t, GLA_PAD)
        p_fox = p_fox.reshape(bsz, t, FOX_PAD)

        y_rw = _rwkv(p_rw, mu, rw_vec, rw_w2p, rw_a2p, rw_g2p, tb)
        y_cv = _conv(p_cv, cv_wp, cv_vec, tb)
        y_gla = _gla(p_gla, gla_a2p, gla_vec, gla_ln_g[l][None, :], tb)

        z_rows = jnp.pad(jnp.swapaxes(p_fox[:, :, 3 * WIDTH:3 * WIDTH + HEADS], 1, 2),
                         ((0, 0), (0, 8 - HEADS), (0, 0)))
        bf = jnp.pad(fox_bf[l], (0, 8 - HEADS))[:, None]
        c_row = _fox_gate(z_rows, bf)
        c_col = jnp.swapaxes(c_row, 1, 2)
        y_fox = _fox(p_fox, c_col, c_row, tb)

        ys = [y.reshape(m, WIDTH) for y in (y_rw, y_cv, y_gla, y_fox)]
        x2 = _merge(x2, ys, w_gate, gate_b[l], ups, w_out[l].astype(BF16), lns[0], alpha, tm)

        w_kv = jnp.concatenate([xa_wk[l], xa_wv[l]], axis=1).astype(BF16)
        kv = _matmul(mem2, w_kv, BF16, min(512, bsz * mlen), 1024).reshape(bsz, mlen, 2 * d)
        x3 = _xattn(x2.reshape(bsz, t, d), kv, xa_wq[l].astype(BF16), xa_wo[l].astype(BF16),
                    lns[1], alpha, tm)

        x2 = _ffn(x3.reshape(m, d), ffn_w1[l].astype(BF16), ffn_w3[l].astype(BF16),
                  ffn_w2[l].astype(BF16), lns[2], alpha, tm)
        x = x2.reshape(bsz, t, d)
    return x
```

```python
import functools

import jax
import jax.numpy as jnp
from jax import lax
from jax.experimental import pallas as pl
from jax.experimental.pallas import tpu as pltpu

F32 = jnp.float32
BF16 = jnp.bfloat16

D_MODEL = 1024
WIDTH = 256
HEADS = 4
HEAD_DIM = WIDTH // HEADS
CHUNK = 64
RW_COLS = 1056
RW_PAD = 1152
CV_COLS = 512
GLA_COLS = 784
GLA_PAD = 896
GLA_DK = 32
FOX_COLS = 772
FOX_ZROWS = 16
FOX_T_ROWS = 2 * WIDTH + FOX_ZROWS
SMALL_COLS = RW_PAD + CV_COLS + GLA_PAD + WIDTH
GATE_COLS = 4 * D_MODEL
CONV_WIDTH = 31
CONV_HALO = 32
D_FF = 2816
FF_CHUNK = 1408
RW_LN_EPS = 64e-5
GLA_EPS = 1e-5
GLA_TAU = 16.0
LN_EPS = 1e-5
NEG_BIG = -1e30
VMEM_LIMIT = 56 * 1024 * 1024


def _bdot(a, b):
    return jnp.dot(a.astype(BF16), b.astype(BF16), preferred_element_type=F32)


def _bdot_nt(a, b):
    return lax.dot_general(a.astype(BF16), b.astype(BF16), (((1,), (1,)), ((), ())),
                           preferred_element_type=F32)


def _bdot_tn(a, b):
    return lax.dot_general(a.astype(BF16), b.astype(BF16), (((0,), (0,)), ((), ())),
                           preferred_element_type=F32)


def _split3(x):
    hi = x.astype(BF16)
    r1 = x - hi.astype(F32)
    mid = r1.astype(BF16)
    lo = (r1 - mid.astype(F32)).astype(BF16)
    return hi, mid, lo


def _dot3_right(x, m):
    return sum(jnp.dot(p, m, preferred_element_type=F32) for p in _split3(x))


def _dot3_left(m, x):
    return sum(jnp.dot(m, p, preferred_element_type=F32) for p in _split3(x))


def _softplus(z):
    return jnp.maximum(z, 0.0) + jnp.log1p(jnp.exp(-jnp.abs(z)))


def _log_sigmoid(z):
    return -_softplus(-z)


def _sigmoid(z):
    return jax.nn.sigmoid(z)


def _layer_norm(z, g, b):
    mu = jnp.mean(z, axis=-1, keepdims=True)
    d = z - mu
    var = jnp.mean(d * d, axis=-1, keepdims=True)
    return d * lax.rsqrt(var + LN_EPS) * g + b


def _head_masks(rows, row_group, cols, col_group):
    ri = lax.broadcasted_iota(jnp.int32, (rows, cols), 0)
    ci = lax.broadcasted_iota(jnp.int32, (rows, cols), 1)
    same = (ri // row_group) == (ci // col_group)
    return same, ri, ci


def _stack4(x):
    return jnp.concatenate([x, x, x, x], axis=0)


def _fold4(x, n):
    return x[0:n] + x[n:2 * n] + x[2 * n:3 * n] + x[3 * n:4 * n]


def _const_spec(shape):
    nd = len(shape)
    return pl.BlockSpec(shape, lambda *_: (0,) * nd)


def _params(sem):
    return pltpu.CompilerParams(dimension_semantics=sem, vmem_limit_bytes=VMEM_LIMIT)


def _mm_kernel(a_ref, w_ref, o_ref):
    o_ref[...] = _bdot(a_ref[...], w_ref[...]).astype(o_ref.dtype)


def _matmul(a, w, out_dtype, tm, tn):
    m, k = a.shape
    n = w.shape[1]
    return pl.pallas_call(
        _mm_kernel,
        grid=(m // tm, n // tn),
        in_specs=[pl.BlockSpec((tm, k), lambda i, j: (i, 0)),
                  pl.BlockSpec((k, tn), lambda i, j: (0, j))],
        out_specs=pl.BlockSpec((tm, tn), lambda i, j: (i, j)),
        out_shape=jax.ShapeDtypeStruct((m, n), out_dtype),
        compiler_params=_params(("parallel", "parallel")),
        name="matmul",
    )(a, w)


def _cast_kernel(w_ref, o_ref):
    o_ref[...] = w_ref[...].astype(o_ref.dtype)


def _to_bf16(w):
    r, c = w.shape
    tr = r if r * c <= 2 * 1024 * 1024 else 256
    return pl.pallas_call(
        _cast_kernel,
        grid=(r // tr,),
        in_specs=[pl.BlockSpec((tr, c), lambda i: (i, 0))],
        out_specs=pl.BlockSpec((tr, c), lambda i: (i, 0)),
        out_shape=jax.ShapeDtypeStruct((r, c), BF16),
        compiler_params=_params(("parallel",)),
        name="cast_bf16",
    )(w)


def _proj_kernel(x_ref, w_ref, wt_ref, rw_ref, cv_ref, gla_ref, fk_ref, fqt_ref, fvt_ref, fzt_ref):
    xb = x_ref[...].astype(BF16)
    off = 0
    for ref, n in ((rw_ref, RW_PAD), (cv_ref, CV_COLS), (gla_ref, GLA_PAD), (fk_ref, WIDTH)):
        ref[...] = jnp.dot(xb, w_ref[:, off:off + n], preferred_element_type=F32)
        off += n
    fqt_ref[...] = _bdot_nt(wt_ref[0:WIDTH, :], xb)
    fvt_ref[...] = _bdot_nt(wt_ref[WIDTH:2 * WIDTH, :], xb)
    fzt_ref[...] = _bdot_nt(wt_ref[2 * WIDTH:FOX_T_ROWS, :], xb)[0:8]


def _in_proj(x2, w_small, w_fox_t, tm):
    m = x2.shape[0]
    widths = (RW_PAD, CV_COLS, GLA_PAD, WIDTH)
    heights = (WIDTH, WIDTH, 8)
    return pl.pallas_call(
        _proj_kernel,
        grid=(m // tm,),
        in_specs=[pl.BlockSpec((tm, D_MODEL), lambda i: (i, 0)),
                  _const_spec((D_MODEL, SMALL_COLS)), _const_spec((FOX_T_ROWS, D_MODEL))],
        out_specs=[pl.BlockSpec((tm, n), lambda i: (i, 0)) for n in widths]
        + [pl.BlockSpec((r, tm), lambda i: (0, i)) for r in heights],
        out_shape=[jax.ShapeDtypeStruct((m, n), F32) for n in widths]
        + [jax.ShapeDtypeStruct((r, m), F32) for r in heights],
        compiler_params=_params(("parallel",)),
        name="in_proj",
    )(x2, w_small, w_fox_t)


def _rwkv_kernel(p_ref, mu_ref, vec_ref, w2_ref, a2_ref, g2_ref, o_ref, st_ref, prev_ref, *, tb):
    c = WIDTH
    n = CHUNK

    @pl.when(pl.program_id(1) == 0)
    def _():
        st_ref[...] = jnp.zeros_like(st_ref)
        prev_ref[...] = jnp.zeros_like(prev_ref)

    p = p_ref[0]
    row = lax.broadcasted_iota(jnp.int32, (tb, 1), 0)
    prev = jnp.where(row == 0, prev_ref[...], pltpu.roll(p, 1, axis=0))
    prev_ref[...] = p[tb - 1:tb, :]
    ps = p + (prev - p) * mu_ref[...]

    w0, a0, k_k, k_a = vec_ref[0:1], vec_ref[1:2], vec_ref[2:3], vec_ref[3:4]
    r_k, ln_g, ln_b = vec_ref[4:5], vec_ref[5:6], vec_ref[6:7]

    r = ps[:, 0:c]
    k = ps[:, c:2 * c]
    v = ps[:, 2 * c:3 * c]
    wa = ps[:, 3 * c:3 * c + 128]
    xg = ps[:, 3 * c + 128:3 * c + 384]

    w_log = -_softplus(-(w0 + _bdot(jnp.tanh(wa), w2_ref[...]))) - 0.5
    logw = -jnp.exp(w_log)
    a_sig = _sigmoid(a0 + _bdot(wa, a2_ref[...]))
    g = _bdot(_sigmoid(xg), g2_ref[...])

    same, ri, ci = _head_masks(c, n, c, n)
    tr, tc = ri % n, ci % n
    m_strict = same & (tr > tc)
    m_incl = same & (tr >= tc)
    ones_bd = jnp.where(same, 1.0, 0.0).astype(BF16)
    tril_bd = jnp.where(m_incl, 1.0, 0.0).astype(BF16)
    eye = jnp.where(same & (tr == tc), 1.0, 0.0)

    kk = k * k_k
    kk = kk / jnp.maximum(jnp.sqrt(_dot3_right(kk * kk, ones_bd)), 1e-12)
    k_mod = k * (1.0 + (a_sig - 1.0) * k_a)
    a_vec = -kk
    b_vec = kk * a_sig

    bd = lambda x: jnp.where(same, _stack4(x), 0.0)

    ys = []
    for ch in range(tb // n):
        sl = slice(ch * n, (ch + 1) * n)
        lw = logw[sl]
        cum = _dot3_left(tril_bd[0:n, 0:n], lw)
        last = cum[n - 1:n, :]
        g_in = jnp.exp(cum)
        g_inv = jnp.exp(-cum)
        g_end = jnp.exp(last - cum)
        at = a_vec[sl] * jnp.exp(cum - lw)
        rt = r[sl] * g_in
        bt = b_vec[sl] * g_inv
        kt = k_mod[sl] * g_inv
        bh = b_vec[sl] * g_end
        kh = k_mod[sl] * g_end
        vc = v[sl]

        lhs = jnp.concatenate([bd(at), bd(rt)], axis=0).astype(BF16)
        rhs = jnp.concatenate([_stack4(bt), _stack4(kt)], axis=0).astype(BF16)
        sc = _bdot_nt(lhs, rhs)
        m_ab = jnp.where(m_strict, sc[0:c, 0:c], 0.0)
        m_ak = jnp.where(m_strict, sc[0:c, c:2 * c], 0.0)
        m_rb = jnp.where(m_incl, sc[c:2 * c, 0:c], 0.0)
        m_rk = jnp.where(m_incl, sc[c:2 * c, c:2 * c], 0.0)

        t_inv = eye + m_ab
        m_pow = m_ab
        for _ in range(5):
            m_pow = _bdot(m_pow, m_pow)
            t_inv = t_inv + _bdot(m_pow, t_inv)

        st = st_ref[...]
        ws = _bdot_nt(lhs, st)
        bdv = bd(vc)
        u_bd = _bdot(t_inv, ws[0:c] + _bdot(m_ak, bdv))
        y_bd = ws[c:2 * c] + _bdot(m_rb, u_bd) + _bdot(m_rk, bdv)
        ys.append(_fold4(y_bd, n))
        u = _fold4(u_bd, n)
        upd = _bdot_tn(jnp.concatenate([u, vc], axis=0), jnp.concatenate([bh, kh], axis=0))
        st_ref[...] = st * jnp.exp(last) + jnp.where(same, upd, 0.0)

    y = jnp.concatenate(ys, axis=0)
    inv_n = 1.0 / HEAD_DIM
    mu_y = _dot3_right(y, ones_bd) * inv_n
    dy = y - mu_y
    var_y = _dot3_right(dy * dy, ones_bd) * inv_n
    yn = dy * lax.rsqrt(var_y + RW_LN_EPS) * ln_g + ln_b
    bonus = _dot3_right(r * k_mod * r_k, ones_bd) * v
    o_ref[0] = (yn + bonus) * g


def _rwkv(p_rw, mu, vec, w2, a2, g2, tb):
    bsz, t, _ = p_rw.shape
    return pl.pallas_call(
        functools.partial(_rwkv_kernel, tb=tb),
        grid=(bsz, t // tb),
        in_specs=[pl.BlockSpec((1, tb, RW_PAD), lambda b, i: (b, i, 0)),
                  _const_spec((1, RW_PAD)), _const_spec((8, WIDTH)),
                  _const_spec((128, WIDTH)), _const_spec((128, WIDTH)), _const_spec((WIDTH, WIDTH))],
        out_specs=pl.BlockSpec((1, tb, WIDTH), lambda b, i: (b, i, 0)),
        out_shape=jax.ShapeDtypeStruct((bsz, t, WIDTH), F32),
        scratch_shapes=[pltpu.VMEM((WIDTH, WIDTH), F32), pltpu.VMEM((1, RW_PAD), F32)],
        compiler_params=_params(("parallel", "arbitrary")),
        name="rwkv7",
    )(p_rw, mu, vec, w2, a2, g2)


def _conv_kernel(p_ref, w_ref, vec_ref, o_ref, buf_ref, *, tb):
    @pl.when(pl.program_id(1) == 0)
    def _():
        buf_ref[0:CONV_HALO, :] = jnp.zeros((CONV_HALO, WIDTH), F32)

    p = p_ref[0]
    buf_ref[CONV_HALO:CONV_HALO + tb, :] = p[:, 0:WIDTH] * _sigmoid(p[:, WIDTH:2 * WIDTH])
    acc = jnp.zeros((tb, WIDTH), F32) + vec_ref[0:1]
    first = CONV_HALO - (CONV_WIDTH - 1)
    for kx in range(CONV_WIDTH):
        acc = acc + buf_ref[first + kx:first + kx + tb, :] * w_ref[kx:kx + 1, :]
    buf_ref[0:CONV_HALO, :] = buf_ref[tb:tb + CONV_HALO, :]
    u = _layer_norm(acc, vec_ref[1:2], vec_ref[2:3])
    o_ref[0] = u * _sigmoid(u)


def _conv(p_cv, w, vec, tb):
    bsz, t, _ = p_cv.shape
    return pl.pallas_call(
        functools.partial(_conv_kernel, tb=tb),
        grid=(bsz, t // tb),
        in_specs=[pl.BlockSpec((1, tb, CV_COLS), lambda b, i: (b, i, 0)),
                  _const_spec((CONV_HALO, WIDTH)), _const_spec((8, WIDTH))],
        out_specs=pl.BlockSpec((1, tb, WIDTH), lambda b, i: (b, i, 0)),
        out_shape=jax.ShapeDtypeStruct((bsz, t, WIDTH), F32),
        scratch_shapes=[pltpu.VMEM((tb + CONV_HALO, WIDTH), F32)],
        compiler_params=_params(("parallel", "arbitrary")),
        name="conv",
    )(p_cv, w, vec)


def _gla_kernel(p_ref, a2_ref, vec_ref, lng_ref, o_ref, st_ref, *, tb):
    c = WIDTH
    n = CHUNK
    dk = HEADS * GLA_DK

    @pl.when(pl.program_id(1) == 0)
    def _():
        st_ref[...] = jnp.zeros_like(st_ref)

    p = p_ref[0]
    q = p[:, 0:dk] * (GLA_DK ** -0.5)
    k = p[:, dk:2 * dk]
    v = p[:, 2 * dk:2 * dk + c]
    gr = p[:, 2 * dk + c:2 * dk + 2 * c]
    z = p[:, 2 * dk + 2 * c:2 * dk + 2 * c + 128]
    log_a = _log_sigmoid(_bdot(z, a2_ref[...]) + vec_ref[0:1]) / GLA_TAU

    same_q, _, _ = _head_masks(c, n, dk, GLA_DK)
    same, ri, ci = _head_masks(c, n, c, n)
    m_incl = same & ((ri % n) >= (ci % n))
    tril = jnp.where(m_incl[0:n, 0:n], 1.0, 0.0).astype(BF16)
    ones_bd = jnp.where(same, 1.0, 0.0).astype(BF16)

    os_ = []
    for ch in range(tb // n):
        sl = slice(ch * n, (ch + 1) * n)
        cum = _dot3_left(tril, log_a[sl])
        last = cum[n - 1:n, :]
        q_dec = q[sl] * jnp.exp(cum)
        k_inv = k[sl] * jnp.exp(-cum)
        k_end = k[sl] * jnp.exp(last - cum)
        vc = v[sl]
        q_bd = jnp.where(same_q, _stack4(q_dec), 0.0).astype(BF16)
        sc = jnp.where(m_incl, _bdot_nt(q_bd, _stack4(k_inv)), 0.0)
        st = st_ref[...]
        o_bd = _bdot(sc, jnp.where(same, _stack4(vc), 0.0)) + _bdot_nt(q_bd, st)
        os_.append(_fold4(o_bd, n))
        st_ref[...] = st * jnp.exp(last) + jnp.where(same_q, _bdot_tn(vc, k_end), 0.0)

    o = jnp.concatenate(os_, axis=0)
    ms = _dot3_right(o * o, ones_bd) * (1.0 / HEAD_DIM)
    o = o * lax.rsqrt(ms + GLA_EPS) * lng_ref[...]
    o_ref[0] = o * (gr * _sigmoid(gr))


def _gla(p_gla, a2, vec, ln_g, tb):
    bsz, t, _ = p_gla.shape
    return pl.pallas_call(
        functools.partial(_gla_kernel, tb=tb),
        grid=(bsz, t // tb),
        in_specs=[pl.BlockSpec((1, tb, GLA_PAD), lambda b, i: (b, i, 0)),
                  _const_spec((128, 128)), _const_spec((8, 128)), _const_spec((1, WIDTH))],
        out_specs=pl.BlockSpec((1, tb, WIDTH), lambda b, i: (b, i, 0)),
        out_shape=jax.ShapeDtypeStruct((bsz, t, WIDTH), F32),
        scratch_shapes=[pltpu.VMEM((WIDTH, HEADS * GLA_DK), F32)],
        compiler_params=_params(("parallel", "arbitrary")),
        name="gla",
    )(p_gla, a2, vec, ln_g)


def _fox_gate_kernel(z_ref, bf_ref, c_ref):
    x = _log_sigmoid(z_ref[...] + bf_ref[...])
    t = x.shape[1]
    lane = lax.broadcasted_iota(jnp.int32, x.shape, 1)
    sh = 1
    while sh < t:
        x = x + jnp.where(lane >= sh, pltpu.roll(x, sh, axis=1), 0.0)
        sh *= 2
    c_ref[...] = x


def _fox_gate(z_rows, bf, t):
    hp, m = z_rows.shape
    return pl.pallas_call(
        _fox_gate_kernel,
        grid=(m // t,),
        in_specs=[pl.BlockSpec((hp, t), lambda b: (0, b)), _const_spec((hp, 1))],
        out_specs=pl.BlockSpec((hp, t), lambda b: (0, b)),
        out_shape=jax.ShapeDtypeStruct((hp, m), F32),
        compiler_params=_params(("parallel",)),
        name="fox_gate",
    )(z_rows, bf)


def _fox_kernel(qt_ref, k_ref, vt_ref, crow_ref, ccol_ref, o_ref, *, tq):
    c = WIDTH
    i = pl.program_id(1)
    qt = qt_ref[...] * (HEAD_DIM ** -0.5)
    row_head = lax.broadcasted_iota(jnp.int32, (c, 1), 0) // HEAD_DIM
    qtm = jnp.concatenate([jnp.where(row_head == h, qt, 0.0) for h in range(HEADS)], axis=1).astype(BF16)
    cq = [crow_ref[h:h + 1, :] for h in range(HEADS)]
    causal = (lax.broadcasted_iota(jnp.int32, (tq, tq), 0)
              <= lax.broadcasted_iota(jnp.int32, (tq, tq), 1))

    def step(ks, carry, masked):
        kblk = k_ref[pl.ds(ks, tq), :].astype(BF16)
        s_all = jnp.dot(kblk, qtm, preferred_element_type=F32)
        stats = []
        for h in range(HEADS):
            m, l, _ = carry[h]
            ck = ccol_ref[pl.ds(ks, tq), h:h + 1]
            s = s_all[:, h * tq:(h + 1) * tq] - ck
            if masked:
                s = jnp.where(causal, s, NEG_BIG)
            m_new = jnp.maximum(m, jnp.max(s, axis=0, keepdims=True) + cq[h])
            alpha = jnp.exp(m - m_new)
            pr = jnp.exp(s - (m_new - cq[h]))
            l = alpha * l + jnp.sum(pr, axis=0, keepdims=True)
            stats.append((m_new, l, alpha, pr.astype(BF16)))
        new = []
        for h in range(HEADS):
            m_new, l, alpha, pr = stats[h]
            vt_h = vt_ref[h * HEAD_DIM:(h + 1) * HEAD_DIM, pl.ds(ks, tq)].astype(BF16)
            acc = alpha * carry[h][2] + jnp.dot(vt_h, pr, preferred_element_type=F32)
            new.append((m_new, l, acc))
        return tuple(new)

    init = tuple((jnp.full((1, tq), NEG_BIG, F32), jnp.zeros((1, tq), F32),
                  jnp.zeros((HEAD_DIM, tq), F32)) for _ in range(HEADS))
    carry = lax.fori_loop(0, i, lambda kb, cr: step(pl.multiple_of(kb * tq, tq), cr, False), init)
    carry = step(pl.multiple_of(i * tq, tq), carry, True)
    out_t = jnp.concatenate([acc / l for _, l, acc in carry], axis=0)
    o_ref[...] = out_t.T


def _fox(q_t, k, v_t, c_row, c_col, t, tq):
    m = k.shape[0]
    nq = t // tq
    hp = c_row.shape[0]
    return pl.pallas_call(
        functools.partial(_fox_kernel, tq=tq),
        grid=(m // t, nq),
        in_specs=[pl.BlockSpec((WIDTH, tq), lambda b, i: (0, b * nq + i)),
                  pl.BlockSpec((t, WIDTH), lambda b, i: (b, 0)),
                  pl.BlockSpec((WIDTH, t), lambda b, i: (0, b)),
                  pl.BlockSpec((hp, tq), lambda b, i: (0, b * nq + i)),
                  pl.BlockSpec((t, hp), lambda b, i: (b, 0))],
        out_specs=pl.BlockSpec((tq, WIDTH), lambda b, i: (b * nq + i, 0)),
        out_shape=jax.ShapeDtypeStruct((m, WIDTH), F32),
        compiler_params=_params(("parallel", "arbitrary")),
        name="fox",
    )(q_t, k, v_t, c_row, c_col)


def _merge_kernel(x_ref, y0_ref, y1_ref, y2_ref, y3_ref, wg_ref, gb_ref, up_ref, wo_ref, ln_ref,
                  o_ref, *, alpha):
    x = x_ref[...]
    xb = x.astype(BF16)
    merged = None
    for j, y_ref in enumerate((y0_ref, y1_ref, y2_ref, y3_ref)):
        gate = _sigmoid(jnp.dot(xb, wg_ref[:, j * D_MODEL:(j + 1) * D_MODEL],
                                preferred_element_type=F32) + gb_ref[j:j + 1, :])
        term = gate * _bdot(y_ref[...], up_ref[j])
        merged = term if merged is None else merged + term
    z = alpha * x + _bdot(merged, wo_ref[...])
    o_ref[...] = _layer_norm(z, ln_ref[0:1], ln_ref[1:2])


def _merge(x2, ys, wg, gb, ups, wo, ln, alpha, tm):
    m = x2.shape[0]
    row = lambda n: pl.BlockSpec((tm, n), lambda i: (i, 0))
    return pl.pallas_call(
        functools.partial(_merge_kernel, alpha=alpha),
        grid=(m // tm,),
        in_specs=[row(D_MODEL)] + [row(WIDTH)] * 4 + [
            _const_spec((D_MODEL, GATE_COLS)), _const_spec((4, D_MODEL)),
            _const_spec((4, WIDTH, D_MODEL)), _const_spec((D_MODEL, D_MODEL)),
            _const_spec((2, D_MODEL))],
        out_specs=row(D_MODEL),
        out_shape=jax.ShapeDtypeStruct((m, D_MODEL), F32),
        compiler_params=_params(("parallel",)),
        name="merge",
    )(x2, *ys, wg, gb, ups, wo, ln)


def _xattn_kernel(x_ref, k_ref, v_ref, wq_ref, wo_ref, ln_ref, o_ref, *, alpha):
    x = x_ref[0]
    q = _bdot(x, wq_ref[...])
    hd = D_MODEL // HEADS
    outs = []
    for h in range(HEADS):
        sl = slice(h * hd, (h + 1) * hd)
        s = _bdot_nt(q[:, sl], k_ref[0, :, sl]) * (hd ** -0.5)
        e = jnp.exp(s - jnp.max(s, axis=-1, keepdims=True))
        pr = e / jnp.sum(e, axis=-1, keepdims=True)
        outs.append(_bdot(pr, v_ref[0, :, sl]))
    y = _bdot(jnp.concatenate(outs, axis=1), wo_ref[...])
    o_ref[0] = _layer_norm(alpha * x + y, ln_ref[0:1], ln_ref[1:2])


def _xattn(x, kv, wq, wo, ln, alpha, tm):
    bsz, t, _ = x.shape
    mlen = kv.shape[1]
    return pl.pallas_call(
        functools.partial(_xattn_kernel, alpha=alpha),
        grid=(bsz, t // tm),
        in_specs=[pl.BlockSpec((1, tm, D_MODEL), lambda b, i: (b, i, 0)),
                  pl.BlockSpec((1, mlen, D_MODEL), lambda b, i: (b, 0, 0)),
                  pl.BlockSpec((1, mlen, D_MODEL), lambda b, i: (b, 0, 1)),
                  _const_spec((D_MODEL, D_MODEL)), _const_spec((D_MODEL, D_MODEL)),
                  _const_spec((2, D_MODEL))],
        out_specs=pl.BlockSpec((1, tm, D_MODEL), lambda b, i: (b, i, 0)),
        out_shape=jax.ShapeDtypeStruct((bsz, t, D_MODEL), F32),
        compiler_params=_params(("parallel", "parallel")),
        name="xattn",
    )(x, kv, kv, wq, wo, ln)


def _ffn_kernel(x_ref, w1_ref, w3_ref, w2_ref, ln_ref, o_ref, *, alpha):
    x = x_ref[...]
    xb = x.astype(BF16)
    y = None
    for c0 in range(0, D_FF, FF_CHUNK):
        h1 = jnp.dot(xb, w1_ref[:, c0:c0 + FF_CHUNK], preferred_element_type=F32)
        h3 = jnp.dot(xb, w3_ref[:, c0:c0 + FF_CHUNK], preferred_element_type=F32)
        part = _bdot(h1 * _sigmoid(h1) * h3, w2_ref[c0:c0 + FF_CHUNK, :])
        y = part if y is None else y + part
    o_ref[...] = _layer_norm(alpha * x + y, ln_ref[0:1], ln_ref[1:2])


def _ffn(x2, w1, w3, w2, ln, alpha, tm):
    m = x2.shape[0]
    return pl.pallas_call(
        functools.partial(_ffn_kernel, alpha=alpha),
        grid=(m // tm,),
        in_specs=[pl.BlockSpec((tm, D_MODEL), lambda i: (i, 0)),
                  _const_spec((D_MODEL, D_FF)), _const_spec((D_MODEL, D_FF)),
                  _const_spec((D_FF, D_MODEL)), _const_spec((2, D_MODEL))],
        out_specs=pl.BlockSpec((tm, D_MODEL), lambda i: (i, 0)),
        out_shape=jax.ShapeDtypeStruct((m, D_MODEL), F32),
        compiler_params=_params(("parallel",)),
        name="ffn",
    )(x2, w1, w3, w2, ln)


def _pad_cols(a, n):
    return jnp.pad(a, ((0, 0), (0, n - a.shape[1])))


def _pad_rows(a, n):
    return jnp.pad(a, ((0, n - a.shape[0]), (0, 0)))


def _rows8(*vs):
    a = jnp.stack([v.reshape(-1) for v in vs])
    return _pad_rows(a, 8)


def kernel(x, mem, w_in, rw_mu, rw_w0, rw_w2, rw_a0, rw_a2, rw_g2, rw_kk, rw_ka, rw_rk, rw_ln_g, rw_ln_b, rw_up, cv_w, cv_b, cv_ln_g, cv_ln_b, cv_up, gla_a2, gla_ab, gla_ln_g, gla_up, fox_bf, fox_up, gate_b, w_out, xa_wq, xa_wk, xa_wv, xa_wo, ffn_w1, ffn_w3, ffn_w2, ln_g, ln_b):
    bsz, t, d = x.shape
    depth = w_in.shape[0]
    alpha = (2.0 * depth) ** 0.25
    m = bsz * t
    mlen = mem.shape[1]
    tm = min(256, t)
    tb = min(256, t)
    mem2 = mem.reshape(bsz * mlen, d)

    for l in range(depth):
        wl = w_in[l]
        o1 = RW_COLS + CV_COLS
        o2 = o1 + GLA_COLS
        o3 = o2 + FOX_COLS
        w_fq, w_fk = wl[:, o2:o2 + WIDTH], wl[:, o2 + WIDTH:o2 + 2 * WIDTH]
        w_fv, w_fz = wl[:, o2 + 2 * WIDTH:o2 + 3 * WIDTH], wl[:, o2 + 3 * WIDTH:o3]
        w_small = _to_bf16(jnp.concatenate([
            _pad_cols(wl[:, 0:RW_COLS], RW_PAD), wl[:, RW_COLS:o1],
            _pad_cols(wl[:, o1:o2], GLA_PAD), w_fk], axis=1))
        w_fox_t = _to_bf16(jnp.concatenate([w_fq.T, w_fv.T, _pad_rows(w_fz.T, FOX_ZROWS)], axis=0))
        w_gate = _to_bf16(wl[:, o3:])
        rw_vec = _rows8(rw_w0[l], rw_a0[l], rw_kk[l], rw_ka[l], rw_rk[l], rw_ln_g[l], rw_ln_b[l])
        rw_w2p = _pad_rows(rw_w2[l], 128).astype(BF16)
        rw_a2p = jnp.pad(rw_a2[l], ((64, 0), (0, 0))).astype(BF16)
        rw_g2p = _pad_rows(rw_g2[l], WIDTH).astype(BF16)
        mu = _pad_cols(rw_mu[l][None, :], RW_PAD)
        cv_vec = _rows8(cv_b[l], cv_ln_g[l], cv_ln_b[l])
        cv_wp = _pad_rows(cv_w[l], CONV_HALO)
        gla_a2p = _pad_rows(gla_a2[l], 128).astype(BF16)
        gla_vec = _rows8(gla_ab[l])
        ups = _to_bf16(jnp.concatenate([rw_up[l], cv_up[l], gla_up[l], fox_up[l]], axis=0))
        ups = ups.reshape(4, WIDTH, d)
        lns = [jnp.stack([ln_g[l, s], ln_b[l, s]]) for s in range(3)]

        x2 = x.reshape(m, d)
        p_rw, p_cv, p_gla, f_k, f_qt, f_vt, f_zt = _in_proj(x2, w_small, w_fox_t, min(512, m))
        p_rw = p_rw.reshape(bsz, t, RW_PAD)
        p_cv = p_cv.reshape(bsz, t, CV_COLS)
        p_gla = p_gla.reshape(bsz, t, GLA_PAD)

        y_rw = _rwkv(p_rw, mu, rw_vec, rw_w2p, rw_a2p, rw_g2p, tb)
        y_cv = _conv(p_cv, cv_wp, cv_vec, tb)
        y_gla = _gla(p_gla, gla_a2p, gla_vec, gla_ln_g[l][None, :], tb)

        bf = jnp.pad(fox_bf[l], (0, 8 - HEADS))[:, None]
        c_row = _fox_gate(f_zt, bf, t)
        y_fox = _fox(f_qt, f_k, f_vt, c_row, c_row.T, t, tb)

        ys = [y.reshape(m, WIDTH) for y in (y_rw, y_cv, y_gla)] + [y_fox]
        x2 = _merge(x2, ys, w_gate, gate_b[l], ups, _to_bf16(w_out[l]), lns[0], alpha, tm)

        w_kv = _to_bf16(jnp.concatenate([xa_wk[l], xa_wv[l]], axis=1))
        kv = _matmul(mem2, w_kv, BF16, min(512, bsz * mlen), 1024).reshape(bsz, mlen, 2 * d)
        x3 = _xattn(x2.reshape(bsz, t, d), kv, _to_bf16(xa_wq[l]), _to_bf16(xa_wo[l]),
                    lns[1], alpha, tm)

        x2 = _ffn(x3.reshape(m, d), _to_bf16(ffn_w1[l]), _to_bf16(ffn_w3[l]),
                  _to_bf16(ffn_w2[l]), lns[2], alpha, tm)
        x = x2.reshape(bsz, t, d)
    return x
```

```python
import functools

import jax
import jax.numpy as jnp
from jax import lax
from jax.experimental import pallas as pl
from jax.experimental.pallas import tpu as pltpu

F32 = jnp.float32
BF16 = jnp.bfloat16

D_MODEL = 1024
WIDTH = 256
HEADS = 4
HEAD_DIM = WIDTH // HEADS
CHUNK = 64
RW_COLS = 1056
RW_PAD = 1152
CV_COLS = 512
GLA_COLS = 784
GLA_PAD = 896
GLA_DK = 32
FOX_COLS = 772
FOX_ZROWS = 16
FOX_T_ROWS = 2 * WIDTH + FOX_ZROWS
SMALL_COLS = RW_PAD + CV_COLS + GLA_PAD + WIDTH
GATE_COLS = 4 * D_MODEL
CONV_WIDTH = 31
CONV_HALO = 32
D_FF = 2816
FF_CHUNK = 1408
RW_LN_EPS = 64e-5
GLA_EPS = 1e-5
GLA_TAU = 16.0
LN_EPS = 1e-5
NEG_BIG = -1e30
VMEM_LIMIT = 56 * 1024 * 1024


def _bdot(a, b):
    return jnp.dot(a.astype(BF16), b.astype(BF16), preferred_element_type=F32)


def _bdot_nt(a, b):
    return lax.dot_general(a.astype(BF16), b.astype(BF16), (((1,), (1,)), ((), ())),
                           preferred_element_type=F32)


def _bdot_tn(a, b):
    return lax.dot_general(a.astype(BF16), b.astype(BF16), (((0,), (0,)), ((), ())),
                           preferred_element_type=F32)


def _split3(x):
    hi = x.astype(BF16)
    r1 = x - hi.astype(F32)
    mid = r1.astype(BF16)
    lo = (r1 - mid.astype(F32)).astype(BF16)
    return hi, mid, lo


def _dot3_right(x, m):
    return sum(jnp.dot(p, m, preferred_element_type=F32) for p in _split3(x))


def _dot3_left(m, x):
    return sum(jnp.dot(m, p, preferred_element_type=F32) for p in _split3(x))


def _softplus(z):
    return jnp.maximum(z, 0.0) + jnp.log1p(jnp.exp(-jnp.abs(z)))


def _log_sigmoid(z):
    return -_softplus(-z)


def _sigmoid(z):
    return jax.nn.sigmoid(z)


def _layer_norm(z, g, b):
    mu = jnp.mean(z, axis=-1, keepdims=True)
    d = z - mu
    var = jnp.mean(d * d, axis=-1, keepdims=True)
    return d * lax.rsqrt(var + LN_EPS) * g + b


def _head_masks(rows, row_group, cols, col_group):
    ri = lax.broadcasted_iota(jnp.int32, (rows, cols), 0)
    ci = lax.broadcasted_iota(jnp.int32, (rows, cols), 1)
    same = (ri // row_group) == (ci // col_group)
    return same, ri, ci


def _stack4(x):
    return jnp.concatenate([x, x, x, x], axis=0)


def _fold4(x, n):
    return x[0:n] + x[n:2 * n] + x[2 * n:3 * n] + x[3 * n:4 * n]


def _const_spec(shape):
    nd = len(shape)
    return pl.BlockSpec(shape, lambda *_: (0,) * nd)


def _layer_spec(shape, *lead):
    nd = len(shape)
    return pl.BlockSpec((None,) * len(lead) + tuple(shape), lambda *_: tuple(lead) + (0,) * nd)


def _params(sem):
    return pltpu.CompilerParams(dimension_semantics=sem, vmem_limit_bytes=VMEM_LIMIT)


def _mm_kernel(a_ref, w_ref, o_ref):
    o_ref[...] = _bdot(a_ref[...], w_ref[...]).astype(o_ref.dtype)


def _matmul(a, w, out_dtype, tm, tn):
    m, k = a.shape
    n = w.shape[1]
    return pl.pallas_call(
        _mm_kernel,
        grid=(m // tm, n // tn),
        in_specs=[pl.BlockSpec((tm, k), lambda i, j: (i, 0)),
                  pl.BlockSpec((k, tn), lambda i, j: (0, j))],
        out_specs=pl.BlockSpec((tm, tn), lambda i, j: (i, j)),
        out_shape=jax.ShapeDtypeStruct((m, n), out_dtype),
        compiler_params=_params(("parallel", "parallel")),
        name="matmul",
    )(a, w)


def _cast_kernel(w_ref, o_ref):
    o_ref[...] = w_ref[...].astype(o_ref.dtype)


def _to_bf16(w, l):
    _, r, c = w.shape
    tr = r if r * c <= 2 * 1024 * 1024 else 256
    return pl.pallas_call(
        _cast_kernel,
        grid=(r // tr,),
        in_specs=[pl.BlockSpec((None, tr, c), lambda i: (l, i, 0))],
        out_specs=pl.BlockSpec((tr, c), lambda i: (i, 0)),
        out_shape=jax.ShapeDtypeStruct((r, c), BF16),
        compiler_params=_params(("parallel",)),
        name="cast_bf16",
    )(w)


def _split_w_in_kernel(w_ref, small_ref, gate_ref, foxt_ref):
    o_cv = RW_COLS
    o_gla = o_cv + CV_COLS
    o_fq = o_gla + GLA_COLS
    o_fk, o_fv, o_fz = o_fq + WIDTH, o_fq + 2 * WIDTH, o_fq + 3 * WIDTH
    o_gate = o_fq + FOX_COLS

    def cols(start, width, valid):
        blk = w_ref[:, start:start + width]
        if valid < width:
            lane = lax.broadcasted_iota(jnp.int32, blk.shape, 1)
            blk = jnp.where(lane < valid, blk, 0.0)
        return blk

    small_ref[:, 0:RW_PAD] = cols(0, RW_PAD, RW_COLS).astype(BF16)
    small_ref[:, RW_PAD:RW_PAD + CV_COLS] = cols(o_cv, CV_COLS, CV_COLS).astype(BF16)
    small_ref[:, RW_PAD + CV_COLS:RW_PAD + CV_COLS + GLA_PAD] = cols(o_gla, GLA_PAD, GLA_COLS).astype(BF16)
    small_ref[:, SMALL_COLS - WIDTH:SMALL_COLS] = cols(o_fk, WIDTH, WIDTH).astype(BF16)
    gate_ref[...] = cols(o_gate, GATE_COLS, GATE_COLS).astype(BF16)
    foxt_ref[0:WIDTH, :] = cols(o_fq, WIDTH, WIDTH).T.astype(BF16)
    foxt_ref[WIDTH:2 * WIDTH, :] = cols(o_fv, WIDTH, WIDTH).T.astype(BF16)
    foxt_ref[2 * WIDTH:FOX_T_ROWS, :] = cols(o_fz, 128, HEADS).T[0:FOX_ZROWS].astype(BF16)


def _split_w_in(w_in, l):
    _, d, n = w_in.shape
    tr = 256
    return pl.pallas_call(
        _split_w_in_kernel,
        grid=(d // tr,),
        in_specs=[pl.BlockSpec((None, tr, n), lambda i: (l, i, 0))],
        out_specs=[pl.BlockSpec((tr, SMALL_COLS), lambda i: (i, 0)),
                   pl.BlockSpec((tr, GATE_COLS), lambda i: (i, 0)),
                   pl.BlockSpec((FOX_T_ROWS, tr), lambda i: (0, i))],
        out_shape=[jax.ShapeDtypeStruct((d, SMALL_COLS), BF16),
                   jax.ShapeDtypeStruct((d, GATE_COLS), BF16),
                   jax.ShapeDtypeStruct((FOX_T_ROWS, d), BF16)],
        compiler_params=_params(("parallel",)),
        name="split_w_in",
    )(w_in)


def _proj_kernel(x_ref, w_ref, wt_ref, rw_ref, cv_ref, gla_ref, fk_ref, fqt_ref, fvt_ref, fzt_ref):
    xb = x_ref[...].astype(BF16)
    off = 0
    for ref, n in ((rw_ref, RW_PAD), (cv_ref, CV_COLS), (gla_ref, GLA_PAD), (fk_ref, WIDTH)):
        ref[...] = jnp.dot(xb, w_ref[:, off:off + n], preferred_element_type=F32)
        off += n
    fqt_ref[...] = _bdot_nt(wt_ref[0:WIDTH, :], xb)
    fvt_ref[...] = _bdot_nt(wt_ref[WIDTH:2 * WIDTH, :], xb)
    fzt_ref[...] = _bdot_nt(wt_ref[2 * WIDTH:FOX_T_ROWS, :], xb)[0:8]


def _in_proj(x2, w_small, w_fox_t, tm):
    m = x2.shape[0]
    widths = (RW_PAD, CV_COLS, GLA_PAD, WIDTH)
    heights = (WIDTH, WIDTH, 8)
    return pl.pallas_call(
        _proj_kernel,
        grid=(m // tm,),
        in_specs=[pl.BlockSpec((tm, D_MODEL), lambda i: (i, 0)),
                  _const_spec((D_MODEL, SMALL_COLS)), _const_spec((FOX_T_ROWS, D_MODEL))],
        out_specs=[pl.BlockSpec((tm, n), lambda i: (i, 0)) for n in widths]
        + [pl.BlockSpec((r, tm), lambda i: (0, i)) for r in heights],
        out_shape=[jax.ShapeDtypeStruct((m, n), F32) for n in widths]
        + [jax.ShapeDtypeStruct((r, m), F32) for r in heights],
        compiler_params=_params(("parallel",)),
        name="in_proj",
    )(x2, w_small, w_fox_t)


def _rwkv_kernel(p_ref, mu_ref, vec_ref, w2_ref, a2_ref, g2_ref, o_ref, st_ref, prev_ref, *, nb, tb):
    c = WIDTH
    n = CHUNK
    seqs = range(nb)

    @pl.when(pl.program_id(1) == 0)
    def _():
        st_ref[...] = jnp.zeros_like(st_ref)
        prev_ref[...] = jnp.zeros_like(prev_ref)

    row = lax.broadcasted_iota(jnp.int32, (tb, 1), 0)
    shifted = []
    for b in seqs:
        p = p_ref[b]
        prev = jnp.where(row == 0, prev_ref[b], pltpu.roll(p, 1, axis=0))
        prev_ref[b] = p[tb - 1:tb, :]
        shifted.append(p + (prev - p) * mu_ref[...])
    ps = jnp.concatenate(shifted, axis=0)

    w0, a0, k_k, k_a = vec_ref[0:1], vec_ref[1:2], vec_ref[2:3], vec_ref[3:4]
    r_k, ln_g, ln_b = vec_ref[4:5], vec_ref[5:6], vec_ref[6:7]

    r = ps[:, 0:c]
    k = ps[:, c:2 * c]
    v = ps[:, 2 * c:3 * c]
    wa = ps[:, 3 * c:3 * c + 128]
    xg = ps[:, 3 * c + 128:3 * c + 384]

    w_log = -_softplus(-(w0 + _bdot(jnp.tanh(wa), w2_ref[...]))) - 0.5
    logw = -jnp.exp(w_log)
    a_sig = _sigmoid(a0 + _bdot(wa, a2_ref[...]))
    g = _bdot(_sigmoid(xg), g2_ref[...])

    same, ri, ci = _head_masks(c, n, c, n)
    tr, tc = ri % n, ci % n
    same_f = jnp.where(same, 1.0, 0.0)
    ones_bd = same_f.astype(BF16)
    strict_bd = jnp.where(same & (tr > tc), 1.0, 0.0).astype(BF16)
    incl_bd = jnp.where(same & (tr >= tc), 1.0, 0.0).astype(BF16)
    eye = jnp.where(same & (tr == tc), 1.0, 0.0)
    tril = incl_bd[0:n, 0:n]
    sc_mask = jnp.concatenate([jnp.concatenate([strict_bd, strict_bd], axis=1),
                               jnp.concatenate([incl_bd, incl_bd], axis=1)], axis=0)

    kk = k * k_k
    kk = kk / jnp.maximum(jnp.sqrt(_dot3_right(kk * kk, ones_bd)), 1e-12)
    k_mod = k * (1.0 + (a_sig - 1.0) * k_a)
    a_vec = -kk
    b_vec = kk * a_sig

    mm = lambda x, y: jnp.dot(x, y, preferred_element_type=F32)
    rep = lambda x: _stack4(x.astype(BF16))
    bd = lambda x: rep(x) * ones_bd

    ys = [[] for _ in seqs]
    for ch in range(tb // n):
        sls = [slice(b * tb + ch * n, b * tb + (ch + 1) * n) for b in seqs]
        lw = [logw[s] for s in sls]
        cum = [_dot3_left(tril, x) for x in lw]
        last = [x[n - 1:n, :] for x in cum]
        g_inv = [jnp.exp(-x) for x in cum]
        g_end = [jnp.exp(l - x) for l, x in zip(last, cum)]
        at = [a_vec[s] * jnp.exp(x - w) for s, x, w in zip(sls, cum, lw)]
        rt = [r[s] * jnp.exp(x) for s, x in zip(sls, cum)]
        bt = [b_vec[s] * gi for s, gi in zip(sls, g_inv)]
        kt = [k_mod[s] * gi for s, gi in zip(sls, g_inv)]
        bh = [b_vec[s] * ge for s, ge in zip(sls, g_end)]
        kh = [k_mod[s] * ge for s, ge in zip(sls, g_end)]

        lhs = [jnp.concatenate([bd(x), bd(y)], axis=0) for x, y in zip(at, rt)]
        rhs = [jnp.concatenate([rep(x), rep(y)], axis=0) for x, y in zip(bt, kt)]
        sc = [_bdot_nt(x, y).astype(BF16) * sc_mask for x, y in zip(lhs, rhs)]
        m_ab = [x[0:c, 0:c] for x in sc]
        m_ak = [x[0:c, c:2 * c] for x in sc]
        m_rb = [x[c:2 * c, 0:c] for x in sc]
        m_rk = [x[c:2 * c, c:2 * c] for x in sc]

        t_inv = [eye + x.astype(F32) for x in m_ab]
        m_pow = m_ab
        for _ in range(5):
            m_pow = [mm(x, x).astype(BF16) for x in m_pow]
            t_inv = [t + mm(x, t.astype(BF16)) for x, t in zip(m_pow, t_inv)]
        t_inv = [t.astype(BF16) for t in t_inv]

        bdv = [bd(v[s]) for s in sls]
        pk = [mm(x, y) for x, y in zip(m_ak, bdv)]
        yk = [mm(x, y) for x, y in zip(m_rk, bdv)]
        st = [st_ref[b] for b in seqs]
        ws = [_bdot_nt(x, s) for x, s in zip(lhs, st)]
        u_bd = [mm(t, (w[0:c] + p_).astype(BF16)) for t, w, p_ in zip(t_inv, ws, pk)]
        y_bd = [w[c:2 * c] + mm(x, u.astype(BF16)) + z for w, x, u, z in zip(ws, m_rb, u_bd, yk)]
        upd = [_bdot_tn(jnp.concatenate([_fold4(u, n), v[s]], axis=0), jnp.concatenate([x, y], axis=0))
               for u, s, x, y in zip(u_bd, sls, bh, kh)]
        for b in seqs:
            ys[b].append(_fold4(y_bd[b], n))
            st_ref[b] = st[b] * jnp.exp(last[b]) + upd[b] * same_f

    y = jnp.concatenate([blk for b in seqs for blk in ys[b]], axis=0)
    inv_n = 1.0 / HEAD_DIM
    mu_y = _dot3_right(y, ones_bd) * inv_n
    dy = y - mu_y
    var_y = _dot3_right(dy * dy, ones_bd) * inv_n
    yn = dy * lax.rsqrt(var_y + RW_LN_EPS) * ln_g + ln_b
    bonus = _dot3_right(r * k_mod * r_k, ones_bd) * v
    out = (yn + bonus) * g
    for b in seqs:
        o_ref[b] = out[b * tb:(b + 1) * tb]


def _rwkv(p_rw, mu, vec, w2, a2, g2, l, nb, tb):
    bsz, t, _ = p_rw.shape
    return pl.pallas_call(
        functools.partial(_rwkv_kernel, nb=nb, tb=tb),
        grid=(bsz // nb, t // tb),
        in_specs=[pl.BlockSpec((nb, tb, RW_PAD), lambda b, i: (b, i, 0)),
                  _layer_spec((1, RW_PAD), l), _layer_spec((8, WIDTH), l),
                  _layer_spec((128, WIDTH), l), _layer_spec((128, WIDTH), l),
                  _layer_spec((WIDTH, WIDTH), l)],
        out_specs=pl.BlockSpec((nb, tb, WIDTH), lambda b, i: (b, i, 0)),
        out_shape=jax.ShapeDtypeStruct((bsz, t, WIDTH), F32),
        scratch_shapes=[pltpu.VMEM((nb, WIDTH, WIDTH), F32), pltpu.VMEM((nb, 1, RW_PAD), F32)],
        compiler_params=_params(("parallel", "arbitrary")),
        name="rwkv7",
    )(p_rw, mu, vec, w2, a2, g2)


def _conv_kernel(p_ref, w_ref, vec_ref, o_ref, buf_ref, sh_ref, *, tb):
    @pl.when(pl.program_id(1) == 0)
    def _():
        buf_ref[0:CONV_HALO, :] = jnp.zeros((CONV_HALO, WIDTH), F32)

    p = p_ref[0]
    buf_ref[CONV_HALO:CONV_HALO + tb, :] = p[:, 0:WIDTH] * _sigmoid(p[:, WIDTH:2 * WIDTH])
    acc = jnp.zeros((tb, WIDTH), F32) + vec_ref[0:1]
    first = CONV_HALO - (CONV_WIDTH - 1)
    for s in range(8):
        last_a = (CONV_WIDTH - 1 - s) // 8
        span = tb + 8 * last_a
        sh_ref[0:span, :] = buf_ref[first + s:first + s + span, :]
        for a in range(last_a + 1):
            kx = 8 * a + s
            acc = acc + sh_ref[8 * a:8 * a + tb, :] * w_ref[kx:kx + 1, :]
    buf_ref[0:CONV_HALO, :] = buf_ref[tb:tb + CONV_HALO, :]
    u = _layer_norm(acc, vec_ref[1:2], vec_ref[2:3])
    o_ref[0] = u * _sigmoid(u)


def _conv(p_cv, w, vec, l, tb):
    bsz, t, _ = p_cv.shape
    return pl.pallas_call(
        functools.partial(_conv_kernel, tb=tb),
        grid=(bsz, t // tb),
        in_specs=[pl.BlockSpec((1, tb, CV_COLS), lambda b, i: (b, i, 0)),
                  _layer_spec((CONV_HALO, WIDTH), l), _layer_spec((8, WIDTH), l)],
        out_specs=pl.BlockSpec((1, tb, WIDTH), lambda b, i: (b, i, 0)),
        out_shape=jax.ShapeDtypeStruct((bsz, t, WIDTH), F32),
        scratch_shapes=[pltpu.VMEM((tb + CONV_HALO, WIDTH), F32), pltpu.VMEM((tb + CONV_HALO, WIDTH), F32)],
        compiler_params=_params(("parallel", "arbitrary")),
        name="conv",
    )(p_cv, w, vec)


def _gla_kernel(p_ref, a2_ref, vec_ref, lng_ref, o_ref, st_ref, *, nb, tb):
    c = WIDTH
    n = CHUNK
    dk = HEADS * GLA_DK

    seqs = range(nb)

    @pl.when(pl.program_id(1) == 0)
    def _():
        st_ref[...] = jnp.zeros_like(st_ref)

    p = jnp.concatenate([p_ref[b] for b in seqs], axis=0)
    q = p[:, 0:dk] * (GLA_DK ** -0.5)
    k = p[:, dk:2 * dk]
    v = p[:, 2 * dk:2 * dk + c]
    gr = p[:, 2 * dk + c:2 * dk + 2 * c]
    z = p[:, 2 * dk + 2 * c:2 * dk + 2 * c + 128]
    log_a = _log_sigmoid(_bdot(z, a2_ref[...]) + vec_ref[0:1]) / GLA_TAU

    same_q, _, _ = _head_masks(c, n, dk, GLA_DK)
    same, ri, ci = _head_masks(c, n, c, n)
    same_qf = jnp.where(same_q, 1.0, 0.0)
    same_qb = same_qf.astype(BF16)
    ones_bd = jnp.where(same, 1.0, 0.0).astype(BF16)
    incl_bd = jnp.where(same & ((ri % n) >= (ci % n)), 1.0, 0.0).astype(BF16)
    tril = incl_bd[0:n, 0:n]

    mm = lambda x, y: jnp.dot(x, y, preferred_element_type=F32)
    rep = lambda x: _stack4(x.astype(BF16))

    os_ = [[] for _ in seqs]
    for ch in range(tb // n):
        sls = [slice(b * tb + ch * n, b * tb + (ch + 1) * n) for b in seqs]
        cum = [_dot3_left(tril, log_a[s]) for s in sls]
        last = [x[n - 1:n, :] for x in cum]
        q_bd = [rep(q[s] * jnp.exp(x)) * same_qb for s, x in zip(sls, cum)]
        k_inv = [rep(k[s] * jnp.exp(-x)) for s, x in zip(sls, cum)]
        k_end = [k[s] * jnp.exp(l - x) for s, l, x in zip(sls, last, cum)]
        sc = [jnp.where(incl_bd > 0, _bdot_nt(x, y).astype(BF16), 0.0) for x, y in zip(q_bd, k_inv)]
        st = [st_ref[b] for b in seqs]
        o_bd = [mm(x, rep(v[s]) * ones_bd) + _bdot_nt(y, z) for x, s, y, z in zip(sc, sls, q_bd, st)]
        upd = [_bdot_tn(v[s], x) for s, x in zip(sls, k_end)]
        for b in seqs:
            os_[b].append(_fold4(o_bd[b], n))
            st_ref[b] = st[b] * jnp.exp(last[b]) + upd[b] * same_qf

    o = jnp.concatenate([blk for b in seqs for blk in os_[b]], axis=0)
    ms = _dot3_right(o * o, ones_bd) * (1.0 / HEAD_DIM)
    o = o * lax.rsqrt(ms + GLA_EPS) * lng_ref[...]
    out = o * (gr * _sigmoid(gr))
    for b in seqs:
        o_ref[b] = out[b * tb:(b + 1) * tb]


def _gla(p_gla, a2, vec, ln_g, l, nb, tb):
    bsz, t, _ = p_gla.shape
    return pl.pallas_call(
        functools.partial(_gla_kernel, nb=nb, tb=tb),
        grid=(bsz // nb, t // tb),
        in_specs=[pl.BlockSpec((nb, tb, GLA_PAD), lambda b, i: (b, i, 0)),
                  _layer_spec((128, 128), l), _layer_spec((8, 128), l), _layer_spec((1, WIDTH), l)],
        out_specs=pl.BlockSpec((nb, tb, WIDTH), lambda b, i: (b, i, 0)),
        out_shape=jax.ShapeDtypeStruct((bsz, t, WIDTH), F32),
        scratch_shapes=[pltpu.VMEM((nb, WIDTH, HEADS * GLA_DK), F32)],
        compiler_params=_params(("parallel", "arbitrary")),
        name="gla",
    )(p_gla, a2, vec, ln_g)


def _fox_gate_kernel(z_ref, bf_ref, c_ref):
    x = _log_sigmoid(z_ref[...] + bf_ref[...])
    t = x.shape[1]
    lane = lax.broadcasted_iota(jnp.int32, x.shape, 1)
    sh = 1
    while sh < t:
        x = x + jnp.where(lane >= sh, pltpu.roll(x, sh, axis=1), 0.0)
        sh *= 2
    c_ref[...] = x


def _fox_gate(z_rows, bf, l, t):
    hp, m = z_rows.shape
    return pl.pallas_call(
        _fox_gate_kernel,
        grid=(m // t,),
        in_specs=[pl.BlockSpec((hp, t), lambda b: (0, b)), _layer_spec((hp, 1), l)],
        out_specs=pl.BlockSpec((hp, t), lambda b: (0, b)),
        out_shape=jax.ShapeDtypeStruct((hp, m), F32),
        compiler_params=_params(("parallel",)),
        name="fox_gate",
    )(z_rows, bf)


def _fox_kernel(qt_ref, k_ref, vt_ref, crow_ref, ccol_ref, o_ref, *, tq):
    c = WIDTH
    i = pl.program_id(1)
    qt = qt_ref[...] * (HEAD_DIM ** -0.5)
    row_head = lax.broadcasted_iota(jnp.int32, (c, 1), 0) // HEAD_DIM
    qtm = jnp.concatenate([jnp.where(row_head == h, qt, 0.0) for h in range(HEADS)], axis=1).astype(BF16)
    cq = [crow_ref[h:h + 1, :] for h in range(HEADS)]
    causal = (lax.broadcasted_iota(jnp.int32, (tq, tq), 0)
              <= lax.broadcasted_iota(jnp.int32, (tq, tq), 1))

    def step(ks, tk, carry, masked):
        kblk = k_ref[pl.ds(ks, tk), :].astype(BF16)
        s_all = jnp.dot(kblk, qtm, preferred_element_type=F32)
        stats = []
        for h in range(HEADS):
            m, l, _ = carry[h]
            ck = ccol_ref[pl.ds(ks, tk), h:h + 1]
            s = s_all[:, h * tq:(h + 1) * tq] - ck
            if masked:
                s = jnp.where(causal, s, NEG_BIG)
            m_new = jnp.maximum(m, jnp.max(s, axis=0, keepdims=True) + cq[h])
            alpha = jnp.exp(m - m_new)
            pr = jnp.exp(s - (m_new - cq[h]))
            l = alpha * l + jnp.sum(pr, axis=0, keepdims=True)
            stats.append((m_new, l, alpha, pr.astype(BF16)))
        new = []
        for h in range(HEADS):
            m_new, l, alpha, pr = stats[h]
            vt_h = vt_ref[h * HEAD_DIM:(h + 1) * HEAD_DIM, pl.ds(ks, tk)].astype(BF16)
            acc = alpha * carry[h][2] + jnp.dot(vt_h, pr, preferred_element_type=F32)
            new.append((m_new, l, acc))
        return tuple(new)

    init = tuple((jnp.full((1, tq), NEG_BIG, F32), jnp.zeros((1, tq), F32),
                  jnp.zeros((HEAD_DIM, tq), F32)) for _ in range(HEADS))
    pairs = i // 2
    carry = lax.fori_loop(
        0, pairs, lambda kb, cr: step(pl.multiple_of(kb * 2 * tq, 2 * tq), 2 * tq, cr, False), init)
    carry = lax.fori_loop(
        2 * pairs, i, lambda kb, cr: step(pl.multiple_of(kb * tq, tq), tq, cr, False), carry)
    carry = step(pl.multiple_of(i * tq, tq), tq, carry, True)
    out_t = jnp.concatenate([acc / l for _, l, acc in carry], axis=0)
    o_ref[...] = out_t.T


def _fox(q_t, k, v_t, c_row, c_col, t, tq):
    m = k.shape[0]
    nq = t // tq
    hp = c_row.shape[0]
    return pl.pallas_call(
        functools.partial(_fox_kernel, tq=tq),
        grid=(m // t, nq),
        in_specs=[pl.BlockSpec((WIDTH, tq), lambda b, i: (0, b * nq + i)),
                  pl.BlockSpec((t, WIDTH), lambda b, i: (b, 0)),
                  pl.BlockSpec((WIDTH, t), lambda b, i: (0, b)),
                  pl.BlockSpec((hp, tq), lambda b, i: (0, b * nq + i)),
                  pl.BlockSpec((t, hp), lambda b, i: (b, 0))],
        out_specs=pl.BlockSpec((tq, WIDTH), lambda b, i: (b * nq + i, 0)),
        out_shape=jax.ShapeDtypeStruct((m, WIDTH), F32),
        compiler_params=_params(("parallel", "arbitrary")),
        name="fox",
    )(q_t, k, v_t, c_row, c_col)


def _merge_kernel(x_ref, y0_ref, y1_ref, y2_ref, y3_ref, wg_ref, gb_ref, u0_ref, u1_ref, u2_ref, u3_ref,
                  wo_ref, ln_ref, o_ref, *, alpha):
    x = x_ref[...]
    xb = x.astype(BF16)
    merged = None
    branches = ((y0_ref, u0_ref), (y1_ref, u1_ref), (y2_ref, u2_ref), (y3_ref, u3_ref))
    for j, (y_ref, up_ref) in enumerate(branches):
        gate = _sigmoid(jnp.dot(xb, wg_ref[:, j * D_MODEL:(j + 1) * D_MODEL],
                                preferred_element_type=F32) + gb_ref[j:j + 1, :])
        term = gate * _bdot(y_ref[...], up_ref[...])
        merged = term if merged is None else merged + term
    z = alpha * x + _bdot(merged, wo_ref[...])
    o_ref[...] = _layer_norm(z, ln_ref[0:1], ln_ref[1:2])


def _merge(x2, ys, wg, gb, ups, wo, ln, l, alpha, tm):
    m = x2.shape[0]
    row = lambda n: pl.BlockSpec((tm, n), lambda i: (i, 0))
    return pl.pallas_call(
        functools.partial(_merge_kernel, alpha=alpha),
        grid=(m // tm,),
        in_specs=[row(D_MODEL)] + [row(WIDTH)] * 4 + [
            _const_spec((D_MODEL, GATE_COLS)), _layer_spec((4, D_MODEL), l)]
        + [_const_spec((WIDTH, D_MODEL))] * 4 + [
            _const_spec((D_MODEL, D_MODEL)), _layer_spec((2, D_MODEL), l, 0)],
        out_specs=row(D_MODEL),
        out_shape=jax.ShapeDtypeStruct((m, D_MODEL), F32),
        compiler_params=_params(("parallel",)),
        name="merge",
    )(x2, *ys, wg, gb, *ups, wo, ln)


def _xattn_kernel(x_ref, k_ref, v_ref, wq_ref, wo_ref, ln_ref, o_ref, *, alpha):
    x = x_ref[0]
    q = _bdot(x, wq_ref[...])
    hd = D_MODEL // HEADS
    outs = []
    for h in range(HEADS):
        sl = slice(h * hd, (h + 1) * hd)
        s = _bdot_nt(q[:, sl], k_ref[0, :, sl]) * (hd ** -0.5)
        e = jnp.exp(s - jnp.max(s, axis=-1, keepdims=True))
        pr = e / jnp.sum(e, axis=-1, keepdims=True)
        outs.append(_bdot(pr, v_ref[0, :, sl]))
    y = _bdot(jnp.concatenate(outs, axis=1), wo_ref[...])
    o_ref[0] = _layer_norm(alpha * x + y, ln_ref[0:1], ln_ref[1:2])


def _xattn(x, k, v, wq, wo, ln, l, alpha, tm):
    bsz, t, _ = x.shape
    mlen = k.shape[1]
    return pl.pallas_call(
        functools.partial(_xattn_kernel, alpha=alpha),
        grid=(bsz, t // tm),
        in_specs=[pl.BlockSpec((1, tm, D_MODEL), lambda b, i: (b, i, 0)),
                  pl.BlockSpec((1, mlen, D_MODEL), lambda b, i: (b, 0, 0)),
                  pl.BlockSpec((1, mlen, D_MODEL), lambda b, i: (b, 0, 0)),
                  _const_spec((D_MODEL, D_MODEL)), _const_spec((D_MODEL, D_MODEL)),
                  _layer_spec((2, D_MODEL), l, 1)],
        out_specs=pl.BlockSpec((1, tm, D_MODEL), lambda b, i: (b, i, 0)),
        out_shape=jax.ShapeDtypeStruct((bsz, t, D_MODEL), F32),
        compiler_params=_params(("parallel", "parallel")),
        name="xattn",
    )(x, k, v, wq, wo, ln)


def _ffn_kernel(x_ref, w1_ref, w3_ref, w2_ref, ln_ref, o_ref, *, alpha):
    x = x_ref[...]
    xb = x.astype(BF16)
    y = None
    for c0 in range(0, D_FF, FF_CHUNK):
        h1 = jnp.dot(xb, w1_ref[:, c0:c0 + FF_CHUNK], preferred_element_type=F32)
        h3 = jnp.dot(xb, w3_ref[:, c0:c0 + FF_CHUNK], preferred_element_type=F32)
        part = _bdot(h1 * _sigmoid(h1) * h3, w2_ref[c0:c0 + FF_CHUNK, :])
        y = part if y is None else y + part
    o_ref[...] = _layer_norm(alpha * x + y, ln_ref[0:1], ln_ref[1:2])


def _ffn(x2, w1, w3, w2, ln, l, alpha, tm):
    m = x2.shape[0]
    return pl.pallas_call(
        functools.partial(_ffn_kernel, alpha=alpha),
        grid=(m // tm,),
        in_specs=[pl.BlockSpec((tm, D_MODEL), lambda i: (i, 0)),
                  _const_spec((D_MODEL, D_FF)), _const_spec((D_MODEL, D_FF)),
                  _const_spec((D_FF, D_MODEL)), _layer_spec((2, D_MODEL), l, 2)],
        out_specs=pl.BlockSpec((tm, D_MODEL), lambda i: (i, 0)),
        out_shape=jax.ShapeDtypeStruct((m, D_MODEL), F32),
        compiler_params=_params(("parallel",)),
        name="ffn",
    )(x2, w1, w3, w2, ln)


def _pad_axis(a, axis, n, before=0):
    widths = [(0, 0)] * a.ndim
    widths[axis] = (before, n - a.shape[axis] - before)
    return jnp.pad(a, widths)


def _rows8(*vs):
    depth = vs[0].shape[0]
    return _pad_axis(jnp.stack([v.reshape(depth, -1) for v in vs], axis=1), 1, 8)


def kernel(x, mem, w_in, rw_mu, rw_w0, rw_w2, rw_a0, rw_a2, rw_g2, rw_kk, rw_ka, rw_rk, rw_ln_g, rw_ln_b, rw_up, cv_w, cv_b, cv_ln_g, cv_ln_b, cv_up, gla_a2, gla_ab, gla_ln_g, gla_up, fox_bf, fox_up, gate_b, w_out, xa_wq, xa_wk, xa_wv, xa_wo, ffn_w1, ffn_w3, ffn_w2, ln_g, ln_b):
    bsz, t, d = x.shape
    depth = w_in.shape[0]
    alpha = (2.0 * depth) ** 0.25
    m = bsz * t
    mlen = mem.shape[1]
    tm = min(256, t)
    tb = min(256, t)
    nb = max(g for g in (8, 4, 2, 1) if bsz % g == 0)
    mem2 = mem.reshape(bsz * mlen, d)

    rw_vec = _rows8(rw_w0, rw_a0, rw_kk, rw_ka, rw_rk, rw_ln_g, rw_ln_b)
    rw_w2p = _pad_axis(rw_w2, 1, 128).astype(BF16)
    rw_a2p = _pad_axis(rw_a2, 1, 128, before=128 - rw_a2.shape[1]).astype(BF16)
    rw_g2p = _pad_axis(rw_g2, 1, WIDTH).astype(BF16)
    mu = _pad_axis(rw_mu, 1, RW_PAD)[:, None, :]
    cv_vec = _rows8(cv_b, cv_ln_g, cv_ln_b)
    cv_wp = _pad_axis(cv_w, 1, CONV_HALO)
    gla_a2p = _pad_axis(gla_a2, 1, 128).astype(BF16)
    gla_vec = _rows8(gla_ab)
    gla_lng = gla_ln_g[:, None, :]
    bf = _pad_axis(fox_bf, 1, 8)[:, :, None]
    lns = jnp.stack([ln_g, ln_b], axis=2)

    for l in range(depth):
        w_small, w_gate, w_fox_t = _split_w_in(w_in, l)
        x2 = x.reshape(m, d)
        p_rw, p_cv, p_gla, f_k, f_qt, f_vt, f_zt = _in_proj(x2, w_small, w_fox_t, min(512, m))
        p_rw = p_rw.reshape(bsz, t, RW_PAD)
        p_cv = p_cv.reshape(bsz, t, CV_COLS)
        p_gla = p_gla.reshape(bsz, t, GLA_PAD)

        y_rw = _rwkv(p_rw, mu, rw_vec, rw_w2p, rw_a2p, rw_g2p, l, nb, CHUNK)
        y_cv = _conv(p_cv, cv_wp, cv_vec, l, tb)
        y_gla = _gla(p_gla, gla_a2p, gla_vec, gla_lng, l, nb, CHUNK)
        c_row = _fox_gate(f_zt, bf, l, t)
        y_fox = _fox(f_qt, f_k, f_vt, c_row, c_row.T, t, tb)

        ys = [y.reshape(m, WIDTH) for y in (y_rw, y_cv, y_gla)] + [y_fox]
        ups = [_to_bf16(w, l) for w in (rw_up, cv_up, gla_up, fox_up)]
        x2 = _merge(x2, ys, w_gate, gate_b, ups, _to_bf16(w_out, l), lns, l, alpha, tm)

        tkv = min(512, bsz * mlen)
        k_mem = _matmul(mem2, _to_bf16(xa_wk, l), BF16, tkv, d).reshape(bsz, mlen, d)
        v_mem = _matmul(mem2, _to_bf16(xa_wv, l), BF16, tkv, d).reshape(bsz, mlen, d)
        x3 = _xattn(x2.reshape(bsz, t, d), k_mem, v_mem, _to_bf16(xa_wq, l), _to_bf16(xa_wo, l),
                    lns, l, alpha, tm)

        x2 = _ffn(x3.reshape(m, d), _to_bf16(ffn_w1, l), _to_bf16(ffn_w3, l),
                  _to_bf16(ffn_w2, l), lns, l, alpha, tm)
        x = x2.reshape(bsz, t, d)
    return x
```

```python
import functools

import jax
import jax.numpy as jnp
from jax import lax
from jax.experimental import pallas as pl
from jax.experimental.pallas import tpu as pltpu

F32 = jnp.float32
BF16 = jnp.bfloat16

D_MODEL = 1024
WIDTH = 256
HEADS = 4
HEAD_DIM = WIDTH // HEADS
CHUNK = 64
RW_COLS = 1056
RW_PAD = 1152
CV_COLS = 512
GLA_COLS = 784
GLA_PAD = 896
GLA_DK = 32
FOX_COLS = 772
FOX_ZROWS = 16
FOX_T_ROWS = 2 * WIDTH + FOX_ZROWS
SMALL_COLS = RW_PAD + CV_COLS + GLA_PAD + WIDTH
GATE_COLS = 4 * D_MODEL
CONV_WIDTH = 31
CONV_HALO = 32
D_FF = 2816
FF_CHUNK = 1408
RW_LN_EPS = 64e-5
GLA_EPS = 1e-5
GLA_TAU = 16.0
LN_EPS = 1e-5
NEG_BIG = -1e30
VMEM_LIMIT = 56 * 1024 * 1024


def _bdot(a, b):
    return jnp.dot(a.astype(BF16), b.astype(BF16), preferred_element_type=F32)


def _bdot_nt(a, b):
    return lax.dot_general(a.astype(BF16), b.astype(BF16), (((1,), (1,)), ((), ())),
                           preferred_element_type=F32)


def _bdot_tn(a, b):
    return lax.dot_general(a.astype(BF16), b.astype(BF16), (((0,), (0,)), ((), ())),
                           preferred_element_type=F32)


def _split3(x):
    hi = x.astype(BF16)
    r1 = x - hi.astype(F32)
    mid = r1.astype(BF16)
    lo = (r1 - mid.astype(F32)).astype(BF16)
    return hi, mid, lo


def _dot3_right(x, m):
    return sum(jnp.dot(p, m, preferred_element_type=F32) for p in _split3(x))


def _dot3_left(m, x):
    return sum(jnp.dot(m, p, preferred_element_type=F32) for p in _split3(x))


def _softplus(z):
    return jnp.maximum(z, 0.0) + jnp.log1p(jnp.exp(-jnp.abs(z)))


def _log_sigmoid(z):
    return -_softplus(-z)


def _sigmoid(z):
    return jax.nn.sigmoid(z)


def _layer_norm(z, g, b):
    mu = jnp.mean(z, axis=-1, keepdims=True)
    d = z - mu
    var = jnp.mean(d * d, axis=-1, keepdims=True)
    return d * lax.rsqrt(var + LN_EPS) * g + b


def _head_masks(rows, row_group, cols, col_group):
    ri = lax.broadcasted_iota(jnp.int32, (rows, cols), 0)
    ci = lax.broadcasted_iota(jnp.int32, (rows, cols), 1)
    same = (ri // row_group) == (ci // col_group)
    return same, ri, ci


def _stack4(x):
    return jnp.concatenate([x, x, x, x], axis=0)


def _fold4(x, n):
    return x[0:n] + x[n:2 * n] + x[2 * n:3 * n] + x[3 * n:4 * n]


def _layer_spec(shape, *lead):
    nd = len(shape)
    return pl.BlockSpec((None,) * len(lead) + tuple(shape), lambda *_: tuple(lead) + (0,) * nd,
                        pipeline_mode=pl.Buffered(1))


def _params(sem):
    return pltpu.CompilerParams(dimension_semantics=sem, vmem_limit_bytes=VMEM_LIMIT)


def _mm_kernel(a_ref, w_ref, o_ref):
    o_ref[...] = _bdot(a_ref[...], w_ref[...]).astype(o_ref.dtype)


def _matmul(a, w, l, out_dtype, tm):
    m, k = a.shape
    n = w.shape[2]
    return pl.pallas_call(
        _mm_kernel,
        grid=(m // tm,),
        in_specs=[pl.BlockSpec((tm, k), lambda i: (i, 0)), _layer_spec((k, n), l)],
        out_specs=pl.BlockSpec((tm, n), lambda i: (i, 0)),
        out_shape=jax.ShapeDtypeStruct((m, n), out_dtype),
        compiler_params=_params(("parallel",)),
        name="matmul",
    )(a, w)


def _cast_kernel(w_ref, o_ref):
    o_ref[...] = w_ref[...].astype(o_ref.dtype)


def _to_bf16(w):
    depth, r, c = w.shape
    tr = r if r * c <= 2 * 1024 * 1024 else 256
    return pl.pallas_call(
        _cast_kernel,
        grid=(depth, r // tr),
        in_specs=[pl.BlockSpec((None, tr, c), lambda l, i: (l, i, 0))],
        out_specs=pl.BlockSpec((None, tr, c), lambda l, i: (l, i, 0)),
        out_shape=jax.ShapeDtypeStruct((depth, r, c), BF16),
        compiler_params=_params(("parallel", "parallel")),
        name="cast_bf16",
    )(w)


def _split_w_in_kernel(w_ref, small_ref, gate_ref, foxt_ref):
    o_cv = RW_COLS
    o_gla = o_cv + CV_COLS
    o_fq = o_gla + GLA_COLS
    o_fk, o_fv, o_fz = o_fq + WIDTH, o_fq + 2 * WIDTH, o_fq + 3 * WIDTH
    o_gate = o_fq + FOX_COLS

    def cols(start, width, valid):
        blk = w_ref[:, start:start + width]
        if valid < width:
            lane = lax.broadcasted_iota(jnp.int32, blk.shape, 1)
            blk = jnp.where(lane < valid, blk, 0.0)
        return blk

    small_ref[:, 0:RW_PAD] = cols(0, RW_PAD, RW_COLS).astype(BF16)
    small_ref[:, RW_PAD:RW_PAD + CV_COLS] = cols(o_cv, CV_COLS, CV_COLS).astype(BF16)
    small_ref[:, RW_PAD + CV_COLS:RW_PAD + CV_COLS + GLA_PAD] = cols(o_gla, GLA_PAD, GLA_COLS).astype(BF16)
    small_ref[:, SMALL_COLS - WIDTH:SMALL_COLS] = cols(o_fk, WIDTH, WIDTH).astype(BF16)
    gate_ref[...] = cols(o_gate, GATE_COLS, GATE_COLS).astype(BF16)
    foxt_ref[0:WIDTH, :] = cols(o_fq, WIDTH, WIDTH).T.astype(BF16)
    foxt_ref[WIDTH:2 * WIDTH, :] = cols(o_fv, WIDTH, WIDTH).T.astype(BF16)
    foxt_ref[2 * WIDTH:FOX_T_ROWS, :] = cols(o_fz, 128, HEADS).T[0:FOX_ZROWS].astype(BF16)


def _split_w_in(w_in):
    depth, d, n = w_in.shape
    tr = 256
    return pl.pallas_call(
        _split_w_in_kernel,
        grid=(depth, d // tr),
        in_specs=[pl.BlockSpec((None, tr, n), lambda l, i: (l, i, 0))],
        out_specs=[pl.BlockSpec((None, tr, SMALL_COLS), lambda l, i: (l, i, 0)),
                   pl.BlockSpec((None, tr, GATE_COLS), lambda l, i: (l, i, 0)),
                   pl.BlockSpec((None, FOX_T_ROWS, tr), lambda l, i: (l, 0, i))],
        out_shape=[jax.ShapeDtypeStruct((depth, d, SMALL_COLS), BF16),
                   jax.ShapeDtypeStruct((depth, d, GATE_COLS), BF16),
                   jax.ShapeDtypeStruct((depth, FOX_T_ROWS, d), BF16)],
        compiler_params=_params(("parallel", "parallel")),
        name="split_w_in",
    )(w_in)


def _proj_kernel(x_ref, w_ref, wt_ref, rw_ref, cv_ref, gla_ref, fk_ref, fqt_ref, fvt_ref, fzt_ref):
    xb = x_ref[...].astype(BF16)
    off = 0
    for ref, n in ((rw_ref, RW_PAD), (cv_ref, CV_COLS), (gla_ref, GLA_PAD), (fk_ref, WIDTH)):
        ref[...] = jnp.dot(xb, w_ref[:, off:off + n], preferred_element_type=F32)
        off += n
    fqt_ref[...] = _bdot_nt(wt_ref[0:WIDTH, :], xb)
    fvt_ref[...] = _bdot_nt(wt_ref[WIDTH:2 * WIDTH, :], xb)
    fzt_ref[...] = _bdot_nt(wt_ref[2 * WIDTH:FOX_T_ROWS, :], xb)[0:8]


def _in_proj(x2, w_small, w_fox_t, l, tm):
    m = x2.shape[0]
    widths = (RW_PAD, CV_COLS, GLA_PAD, WIDTH)
    heights = (WIDTH, WIDTH, 8)
    return pl.pallas_call(
        _proj_kernel,
        grid=(m // tm,),
        in_specs=[pl.BlockSpec((tm, D_MODEL), lambda i: (i, 0)),
                  _layer_spec((D_MODEL, SMALL_COLS), l), _layer_spec((FOX_T_ROWS, D_MODEL), l)],
        out_specs=[pl.BlockSpec((tm, n), lambda i: (i, 0)) for n in widths]
        + [pl.BlockSpec((r, tm), lambda i: (0, i)) for r in heights],
        out_shape=[jax.ShapeDtypeStruct((m, n), F32) for n in widths]
        + [jax.ShapeDtypeStruct((r, m), F32) for r in heights],
        compiler_params=_params(("parallel",)),
        name="in_proj",
    )(x2, w_small, w_fox_t)


def _rwkv_kernel(p_ref, mu_ref, vec_ref, w2_ref, a2_ref, g2_ref, o_ref, st_ref, prev_ref, *, nb, tb):
    c = WIDTH
    n = CHUNK
    seqs = range(nb)

    @pl.when(pl.program_id(1) == 0)
    def _():
        st_ref[...] = jnp.zeros_like(st_ref)
        prev_ref[...] = jnp.zeros_like(prev_ref)

    row = lax.broadcasted_iota(jnp.int32, (tb, 1), 0)
    shifted = []
    for b in seqs:
        p = p_ref[b]
        prev = jnp.where(row == 0, prev_ref[b], pltpu.roll(p, 1, axis=0))
        prev_ref[b] = p[tb - 1:tb, :]
        shifted.append(p + (prev - p) * mu_ref[...])
    ps = jnp.concatenate(shifted, axis=0)

    w0, a0, k_k, k_a = vec_ref[0:1], vec_ref[1:2], vec_ref[2:3], vec_ref[3:4]
    r_k, ln_g, ln_b = vec_ref[4:5], vec_ref[5:6], vec_ref[6:7]

    r = ps[:, 0:c]
    k = ps[:, c:2 * c]
    v = ps[:, 2 * c:3 * c]
    wa = ps[:, 3 * c:3 * c + 128]
    xg = ps[:, 3 * c + 128:3 * c + 384]

    w_log = -_softplus(-(w0 + _bdot(jnp.tanh(wa), w2_ref[...]))) - 0.5
    logw = -jnp.exp(w_log)
    a_sig = _sigmoid(a0 + _bdot(wa, a2_ref[...]))
    g = _bdot(_sigmoid(xg), g2_ref[...])

    same, ri, ci = _head_masks(c, n, c, n)
    tr, tc = ri % n, ci % n
    same_f = jnp.where(same, 1.0, 0.0)
    ones_bd = same_f.astype(BF16)
    strict_bd = jnp.where(same & (tr > tc), 1.0, 0.0).astype(BF16)
    incl_bd = jnp.where(same & (tr >= tc), 1.0, 0.0).astype(BF16)
    eye = jnp.where(same & (tr == tc), 1.0, 0.0)
    tril = incl_bd[0:n, 0:n]
    sc_mask = jnp.concatenate([jnp.concatenate([strict_bd, strict_bd], axis=1),
                               jnp.concatenate([incl_bd, incl_bd], axis=1)], axis=0)

    kk = k * k_k
    kk = kk / jnp.maximum(jnp.sqrt(_dot3_right(kk * kk, ones_bd)), 1e-12)
    k_mod = k * (1.0 + (a_sig - 1.0) * k_a)
    a_vec = -kk
    b_vec = kk * a_sig

    mm = lambda x, y: jnp.dot(x, y, preferred_element_type=F32)
    rep = lambda x: _stack4(x.astype(BF16))
    bd = lambda x: rep(x) * ones_bd

    ys = [[] for _ in seqs]
    for ch in range(tb // n):
        sls = [slice(b * tb + ch * n, b * tb + (ch + 1) * n) for b in seqs]
        lw = [logw[s] for s in sls]
        cum = [_dot3_left(tril, x) for x in lw]
        last = [x[n - 1:n, :] for x in cum]
        g_inv = [jnp.exp(-x) for x in cum]
        g_end = [jnp.exp(l - x) for l, x in zip(last, cum)]
        at = [a_vec[s] * jnp.exp(x - w) for s, x, w in zip(sls, cum, lw)]
        rt = [r[s] * jnp.exp(x) for s, x in zip(sls, cum)]
        bt = [b_vec[s] * gi for s, gi in zip(sls, g_inv)]
        kt = [k_mod[s] * gi for s, gi in zip(sls, g_inv)]
        bh = [b_vec[s] * ge for s, ge in zip(sls, g_end)]
        kh = [k_mod[s] * ge for s, ge in zip(sls, g_end)]

        lhs = [jnp.concatenate([bd(x), bd(y)], axis=0) for x, y in zip(at, rt)]
        rhs = [jnp.concatenate([rep(x), rep(y)], axis=0) for x, y in zip(bt, kt)]
        sc = [_bdot_nt(x, y).astype(BF16) * sc_mask for x, y in zip(lhs, rhs)]
        m_ab = [x[0:c, 0:c] for x in sc]
        m_ak = [x[0:c, c:2 * c] for x in sc]
        m_rb = [x[c:2 * c, 0:c] for x in sc]
        m_rk = [x[c:2 * c, c:2 * c] for x in sc]

        t_inv = [eye + x.astype(F32) for x in m_ab]
        m_pow = m_ab
        for _ in range(5):
            m_pow = [mm(x, x).astype(BF16) for x in m_pow]
            t_inv = [t + mm(x, t.astype(BF16)) for x, t in zip(m_pow, t_inv)]
        t_inv = [t.astype(BF16) for t in t_inv]

        bdv = [bd(v[s]) for s in sls]
        pk = [mm(x, y) for x, y in zip(m_ak, bdv)]
        yk = [mm(x, y) for x, y in zip(m_rk, bdv)]
        st = [st_ref[b] for b in seqs]
        ws = [_bdot_nt(x, s) for x, s in zip(lhs, st)]
        u_bd = [mm(t, (w[0:c] + p_).astype(BF16)) for t, w, p_ in zip(t_inv, ws, pk)]
        y_bd = [w[c:2 * c] + mm(x, u.astype(BF16)) + z for w, x, u, z in zip(ws, m_rb, u_bd, yk)]
        upd = [_bdot_tn(jnp.concatenate([_fold4(u, n), v[s]], axis=0), jnp.concatenate([x, y], axis=0))
               for u, s, x, y in zip(u_bd, sls, bh, kh)]
        for b in seqs:
            ys[b].append(_fold4(y_bd[b], n))
            st_ref[b] = st[b] * jnp.exp(last[b]) + upd[b] * same_f

    y = jnp.concatenate([blk for b in seqs for blk in ys[b]], axis=0)
    inv_n = 1.0 / HEAD_DIM
    mu_y = _dot3_right(y, ones_bd) * inv_n
    dy = y - mu_y
    var_y = _dot3_right(dy * dy, ones_bd) * inv_n
    yn = dy * lax.rsqrt(var_y + RW_LN_EPS) * ln_g + ln_b
    bonus = _dot3_right(r * k_mod * r_k, ones_bd) * v
    out = (yn + bonus) * g
    for b in seqs:
        o_ref[b] = out[b * tb:(b + 1) * tb]


def _rwkv(p_rw, mu, vec, w2, a2, g2, l, nb, tb):
    bsz, t, _ = p_rw.shape
    return pl.pallas_call(
        functools.partial(_rwkv_kernel, nb=nb, tb=tb),
        grid=(bsz // nb, t // tb),
        in_specs=[pl.BlockSpec((nb, tb, RW_PAD), lambda b, i: (b, i, 0)),
                  _layer_spec((1, RW_PAD), l), _layer_spec((8, WIDTH), l),
                  _layer_spec((128, WIDTH), l), _layer_spec((128, WIDTH), l),
                  _layer_spec((WIDTH, WIDTH), l)],
        out_specs=pl.BlockSpec((nb, tb, WIDTH), lambda b, i: (b, i, 0)),
        out_shape=jax.ShapeDtypeStruct((bsz, t, WIDTH), F32),
        scratch_shapes=[pltpu.VMEM((nb, WIDTH, WIDTH), F32), pltpu.VMEM((nb, 1, RW_PAD), F32)],
        compiler_params=_params(("parallel", "arbitrary")),
        name="rwkv7",
    )(p_rw, mu, vec, w2, a2, g2)


def _conv_kernel(p_ref, w_ref, vec_ref, o_ref, buf_ref, sh_ref, *, tb):
    @pl.when(pl.program_id(1) == 0)
    def _():
        buf_ref[0:CONV_HALO, :] = jnp.zeros((CONV_HALO, WIDTH), F32)

    p = p_ref[0]
    buf_ref[CONV_HALO:CONV_HALO + tb, :] = p[:, 0:WIDTH] * _sigmoid(p[:, WIDTH:2 * WIDTH])
    acc = jnp.zeros((tb, WIDTH), F32) + vec_ref[0:1]
    first = CONV_HALO - (CONV_WIDTH - 1)
    for s in range(8):
        last_a = (CONV_WIDTH - 1 - s) // 8
        span = tb + 8 * last_a
        sh_ref[0:span, :] = buf_ref[first + s:first + s + span, :]
        for a in range(last_a + 1):
            kx = 8 * a + s
            acc = acc + sh_ref[8 * a:8 * a + tb, :] * w_ref[kx:kx + 1, :]
    buf_ref[0:CONV_HALO, :] = buf_ref[tb:tb + CONV_HALO, :]
    u = _layer_norm(acc, vec_ref[1:2], vec_ref[2:3])
    o_ref[0] = u * _sigmoid(u)


def _conv(p_cv, w, vec, l, tb):
    bsz, t, _ = p_cv.shape
    return pl.pallas_call(
        functools.partial(_conv_kernel, tb=tb),
        grid=(bsz, t // tb),
        in_specs=[pl.BlockSpec((1, tb, CV_COLS), lambda b, i: (b, i, 0)),
                  _layer_spec((CONV_HALO, WIDTH), l), _layer_spec((8, WIDTH), l)],
        out_specs=pl.BlockSpec((1, tb, WIDTH), lambda b, i: (b, i, 0)),
        out_shape=jax.ShapeDtypeStruct((bsz, t, WIDTH), F32),
        scratch_shapes=[pltpu.VMEM((tb + CONV_HALO, WIDTH), F32), pltpu.VMEM((tb + CONV_HALO, WIDTH), F32)],
        compiler_params=_params(("parallel", "arbitrary")),
        name="conv",
    )(p_cv, w, vec)


def _gla_kernel(p_ref, a2_ref, vec_ref, lng_ref, o_ref, st_ref, *, nb, tb):
    c = WIDTH
    n = CHUNK
    dk = HEADS * GLA_DK

    seqs = range(nb)

    @pl.when(pl.program_id(1) == 0)
    def _():
        st_ref[...] = jnp.zeros_like(st_ref)

    p = jnp.concatenate([p_ref[b] for b in seqs], axis=0)
    q = p[:, 0:dk] * (GLA_DK ** -0.5)
    k = p[:, dk:2 * dk]
    v = p[:, 2 * dk:2 * dk + c]
    gr = p[:, 2 * dk + c:2 * dk + 2 * c]
    z = p[:, 2 * dk + 2 * c:2 * dk + 2 * c + 128]
    log_a = _log_sigmoid(_bdot(z, a2_ref[...]) + vec_ref[0:1]) / GLA_TAU

    same_q, _, _ = _head_masks(c, n, dk, GLA_DK)
    same, ri, ci = _head_masks(c, n, c, n)
    same_qf = jnp.where(same_q, 1.0, 0.0)
    same_qb = same_qf.astype(BF16)
    ones_bd = jnp.where(same, 1.0, 0.0).astype(BF16)
    incl_bd = jnp.where(same & ((ri % n) >= (ci % n)), 1.0, 0.0).astype(BF16)
    tril = incl_bd[0:n, 0:n]

    mm = lambda x, y: jnp.dot(x, y, preferred_element_type=F32)
    rep = lambda x: _stack4(x.astype(BF16))

    os_ = [[] for _ in seqs]
    for ch in range(tb // n):
        sls = [slice(b * tb + ch * n, b * tb + (ch + 1) * n) for b in seqs]
        cum = [_dot3_left(tril, log_a[s]) for s in sls]
        last = [x[n - 1:n, :] for x in cum]
        q_bd = [rep(q[s] * jnp.exp(x)) * same_qb for s, x in zip(sls, cum)]
        k_inv = [rep(k[s] * jnp.exp(-x)) for s, x in zip(sls, cum)]
        k_end = [k[s] * jnp.exp(l - x) for s, l, x in zip(sls, last, cum)]
        sc = [jnp.where(incl_bd > 0, _bdot_nt(x, y).astype(BF16), 0.0) for x, y in zip(q_bd, k_inv)]
        st = [st_ref[b] for b in seqs]
        o_bd = [mm(x, rep(v[s]) * ones_bd) + _bdot_nt(y, z) for x, s, y, z in zip(sc, sls, q_bd, st)]
        upd = [_bdot_tn(v[s], x) for s, x in zip(sls, k_end)]
        for b in seqs:
            os_[b].append(_fold4(o_bd[b], n))
            st_ref[b] = st[b] * jnp.exp(last[b]) + upd[b] * same_qf

    o = jnp.concatenate([blk for b in seqs for blk in os_[b]], axis=0)
    ms = _dot3_right(o * o, ones_bd) * (1.0 / HEAD_DIM)
    o = o * lax.rsqrt(ms + GLA_EPS) * lng_ref[...]
    out = o * (gr * _sigmoid(gr))
    for b in seqs:
        o_ref[b] = out[b * tb:(b + 1) * tb]


def _gla(p_gla, a2, vec, ln_g, l, nb, tb):
    bsz, t, _ = p_gla.shape
    return pl.pallas_call(
        functools.partial(_gla_kernel, nb=nb, tb=tb),
        grid=(bsz // nb, t // tb),
        in_specs=[pl.BlockSpec((nb, tb, GLA_PAD), lambda b, i: (b, i, 0)),
                  _layer_spec((128, 128), l), _layer_spec((8, 128), l), _layer_spec((1, WIDTH), l)],
        out_specs=pl.BlockSpec((nb, tb, WIDTH), lambda b, i: (b, i, 0)),
        out_shape=jax.ShapeDtypeStruct((bsz, t, WIDTH), F32),
        scratch_shapes=[pltpu.VMEM((nb, WIDTH, HEADS * GLA_DK), F32)],
        compiler_params=_params(("parallel", "arbitrary")),
        name="gla",
    )(p_gla, a2, vec, ln_g)


def _fox_gate_kernel(z_ref, bf_ref, c_ref):
    x = _log_sigmoid(z_ref[...] + bf_ref[...])
    t = x.shape[1]
    lane = lax.broadcasted_iota(jnp.int32, x.shape, 1)
    sh = 1
    while sh < t:
        x = x + jnp.where(lane >= sh, pltpu.roll(x, sh, axis=1), 0.0)
        sh *= 2
    c_ref[...] = x


def _fox_gate(z_rows, bf, l, t):
    hp, m = z_rows.shape
    return pl.pallas_call(
        _fox_gate_kernel,
        grid=(m // t,),
        in_specs=[pl.BlockSpec((hp, t), lambda b: (0, b)), _layer_spec((hp, 1), l)],
        out_specs=pl.BlockSpec((hp, t), lambda b: (0, b)),
        out_shape=jax.ShapeDtypeStruct((hp, m), F32),
        compiler_params=_params(("parallel",)),
        name="fox_gate",
    )(z_rows, bf)


def _fox_kernel(qt_ref, k_ref, vt_ref, crow_ref, ccol_ref, o_ref, *, tq):
    c = WIDTH
    i = pl.program_id(1)
    qt = qt_ref[...] * (HEAD_DIM ** -0.5)
    row_head = lax.broadcasted_iota(jnp.int32, (c, 1), 0) // HEAD_DIM
    qtm = jnp.concatenate([jnp.where(row_head == h, qt, 0.0) for h in range(HEADS)], axis=1).astype(BF16)
    cq = [crow_ref[h:h + 1, :] for h in range(HEADS)]
    causal = (lax.broadcasted_iota(jnp.int32, (tq, tq), 0)
              <= lax.broadcasted_iota(jnp.int32, (tq, tq), 1))

    def step(ks, tk, carry, masked):
        kblk = k_ref[pl.ds(ks, tk), :].astype(BF16)
        s_all = jnp.dot(kblk, qtm, preferred_element_type=F32)
        stats = []
        for h in range(HEADS):
            m, l, _ = carry[h]
            ck = ccol_ref[pl.ds(ks, tk), h:h + 1]
            s = s_all[:, h * tq:(h + 1) * tq] - ck
            if masked:
                s = jnp.where(causal, s, NEG_BIG)
            m_new = jnp.maximum(m, jnp.max(s, axis=0, keepdims=True) + cq[h])
            alpha = jnp.exp(m - m_new)
            pr = jnp.exp(s - (m_new - cq[h]))
            l = alpha * l + jnp.sum(pr, axis=0, keepdims=True)
            stats.append((m_new, l, alpha, pr.astype(BF16)))
        new = []
        for h in range(HEADS):
            m_new, l, alpha, pr = stats[h]
            vt_h = vt_ref[h * HEAD_DIM:(h + 1) * HEAD_DIM, pl.ds(ks, tk)].astype(BF16)
            acc = alpha * carry[h][2] + jnp.dot(vt_h, pr, preferred_element_type=F32)
            new.append((m_new, l, acc))
        return tuple(new)

    init = tuple((jnp.full((1, tq), NEG_BIG, F32), jnp.zeros((1, tq), F32),
                  jnp.zeros((HEAD_DIM, tq), F32)) for _ in range(HEADS))
    pairs = i // 2
    carry = lax.fori_loop(
        0, pairs, lambda kb, cr: step(pl.multiple_of(kb * 2 * tq, 2 * tq), 2 * tq, cr, False), init)
    carry = lax.fori_loop(
        2 * pairs, i, lambda kb, cr: step(pl.multiple_of(kb * tq, tq), tq, cr, False), carry)
    carry = step(pl.multiple_of(i * tq, tq), tq, carry, True)
    out_t = jnp.concatenate([acc / l for _, l, acc in carry], axis=0)
    o_ref[...] = out_t.T


def _fox(q_t, k, v_t, c_row, c_col, t, tq):
    m = k.shape[0]
    nq = t // tq
    hp = c_row.shape[0]
    return pl.pallas_call(
        functools.partial(_fox_kernel, tq=tq),
        grid=(m // t, nq),
        in_specs=[pl.BlockSpec((WIDTH, tq), lambda b, i: (0, b * nq + i)),
                  pl.BlockSpec((t, WIDTH), lambda b, i: (b, 0)),
                  pl.BlockSpec((WIDTH, t), lambda b, i: (0, b)),
                  pl.BlockSpec((hp, tq), lambda b, i: (0, b * nq + i)),
                  pl.BlockSpec((t, hp), lambda b, i: (b, 0))],
        out_specs=pl.BlockSpec((tq, WIDTH), lambda b, i: (b * nq + i, 0)),
        out_shape=jax.ShapeDtypeStruct((m, WIDTH), F32),
        compiler_params=_params(("parallel", "arbitrary")),
        name="fox",
    )(q_t, k, v_t, c_row, c_col)


def _merge_kernel(x_ref, y0_ref, y1_ref, y2_ref, y3_ref, wg_ref, gb_ref, u0_ref, u1_ref, u2_ref, u3_ref,
                  wo_ref, ln_ref, o_ref, *, alpha):
    x = x_ref[...]
    xb = x.astype(BF16)
    merged = None
    branches = ((y0_ref, u0_ref), (y1_ref, u1_ref), (y2_ref, u2_ref), (y3_ref, u3_ref))
    for j, (y_ref, up_ref) in enumerate(branches):
        gate = _sigmoid(jnp.dot(xb, wg_ref[:, j * D_MODEL:(j + 1) * D_MODEL],
                                preferred_element_type=F32) + gb_ref[j:j + 1, :])
        term = gate * _bdot(y_ref[...], up_ref[...])
        merged = term if merged is None else merged + term
    z = alpha * x + _bdot(merged, wo_ref[...])
    o_ref[...] = _layer_norm(z, ln_ref[0:1], ln_ref[1:2])


def _merge(x2, ys, wg, gb, ups, wo, ln, l, alpha, tm):
    m = x2.shape[0]
    row = lambda n: pl.BlockSpec((tm, n), lambda i: (i, 0))
    return pl.pallas_call(
        functools.partial(_merge_kernel, alpha=alpha),
        grid=(m // tm,),
        in_specs=[row(D_MODEL)] + [row(WIDTH)] * 4 + [
            _layer_spec((D_MODEL, GATE_COLS), l), _layer_spec((4, D_MODEL), l)]
        + [_layer_spec((WIDTH, D_MODEL), l)] * 4 + [
            _layer_spec((D_MODEL, D_MODEL), l), _layer_spec((2, D_MODEL), l, 0)],
        out_specs=row(D_MODEL),
        out_shape=jax.ShapeDtypeStruct((m, D_MODEL), F32),
        compiler_params=_params(("parallel",)),
        name="merge",
    )(x2, *ys, wg, gb, *ups, wo, ln)


def _xattn_kernel(x_ref, k_ref, v_ref, wq_ref, wo_ref, ln_ref, o_ref, *, alpha):
    x = x_ref[0]
    q = _bdot(x, wq_ref[...])
    hd = D_MODEL // HEADS
    outs = []
    for h in range(HEADS):
        sl = slice(h * hd, (h + 1) * hd)
        s = _bdot_nt(q[:, sl], k_ref[0, :, sl]) * (hd ** -0.5)
        e = jnp.exp(s - jnp.max(s, axis=-1, keepdims=True))
        pr = e / jnp.sum(e, axis=-1, keepdims=True)
        outs.append(_bdot(pr, v_ref[0, :, sl]))
    y = _bdot(jnp.concatenate(outs, axis=1), wo_ref[...])
    o_ref[0] = _layer_norm(alpha * x + y, ln_ref[0:1], ln_ref[1:2])


def _xattn(x, k, v, wq, wo, ln, l, alpha, tm):
    bsz, t, _ = x.shape
    mlen = k.shape[1]
    return pl.pallas_call(
        functools.partial(_xattn_kernel, alpha=alpha),
        grid=(bsz, t // tm),
        in_specs=[pl.BlockSpec((1, tm, D_MODEL), lambda b, i: (b, i, 0)),
                  pl.BlockSpec((1, mlen, D_MODEL), lambda b, i: (b, 0, 0)),
                  pl.BlockSpec((1, mlen, D_MODEL), lambda b, i: (b, 0, 0)),
                  _layer_spec((D_MODEL, D_MODEL), l), _layer_spec((D_MODEL, D_MODEL), l),
                  _layer_spec((2, D_MODEL), l, 1)],
        out_specs=pl.BlockSpec((1, tm, D_MODEL), lambda b, i: (b, i, 0)),
        out_shape=jax.ShapeDtypeStruct((bsz, t, D_MODEL), F32),
        compiler_params=_params(("parallel", "parallel")),
        name="xattn",
    )(x, k, v, wq, wo, ln)


def _ffn_kernel(x_ref, w1_ref, w3_ref, w2_ref, ln_ref, o_ref, *, alpha):
    x = x_ref[...]
    xb = x.astype(BF16)
    y = None
    for c0 in range(0, D_FF, FF_CHUNK):
        h1 = jnp.dot(xb, w1_ref[:, c0:c0 + FF_CHUNK], preferred_element_type=F32)
        h3 = jnp.dot(xb, w3_ref[:, c0:c0 + FF_CHUNK], preferred_element_type=F32)
        part = _bdot(h1 * _sigmoid(h1) * h3, w2_ref[c0:c0 + FF_CHUNK, :])
        y = part if y is None else y + part
    o_ref[...] = _layer_norm(alpha * x + y, ln_ref[0:1], ln_ref[1:2])


def _ffn(x2, w1, w3, w2, ln, l, alpha, tm):
    m = x2.shape[0]
    return pl.pallas_call(
        functools.partial(_ffn_kernel, alpha=alpha),
        grid=(m // tm,),
        in_specs=[pl.BlockSpec((tm, D_MODEL), lambda i: (i, 0)),
                  _layer_spec((D_MODEL, D_FF), l), _layer_spec((D_MODEL, D_FF), l),
                  _layer_spec((D_FF, D_MODEL), l), _layer_spec((2, D_MODEL), l, 2)],
        out_specs=pl.BlockSpec((tm, D_MODEL), lambda i: (i, 0)),
        out_shape=jax.ShapeDtypeStruct((m, D_MODEL), F32),
        compiler_params=_params(("parallel",)),
        name="ffn",
    )(x2, w1, w3, w2, ln)


def _pad_axis(a, axis, n, before=0):
    widths = [(0, 0)] * a.ndim
    widths[axis] = (before, n - a.shape[axis] - before)
    return jnp.pad(a, widths)


def _rows8(*vs):
    depth = vs[0].shape[0]
    return _pad_axis(jnp.stack([v.reshape(depth, -1) for v in vs], axis=1), 1, 8)


def kernel(x, mem, w_in, rw_mu, rw_w0, rw_w2, rw_a0, rw_a2, rw_g2, rw_kk, rw_ka, rw_rk, rw_ln_g, rw_ln_b, rw_up, cv_w, cv_b, cv_ln_g, cv_ln_b, cv_up, gla_a2, gla_ab, gla_ln_g, gla_up, fox_bf, fox_up, gate_b, w_out, xa_wq, xa_wk, xa_wv, xa_wo, ffn_w1, ffn_w3, ffn_w2, ln_g, ln_b):
    bsz, t, d = x.shape
    depth = w_in.shape[0]
    alpha = (2.0 * depth) ** 0.25
    m = bsz * t
    mlen = mem.shape[1]
    tm = min(512, t)
    tb = min(256, t)
    nb = max(g for g in (8, 4, 2, 1) if bsz % g == 0)
    mem2 = mem.reshape(bsz * mlen, d)

    rw_vec = _rows8(rw_w0, rw_a0, rw_kk, rw_ka, rw_rk, rw_ln_g, rw_ln_b)
    rw_w2p = _pad_axis(rw_w2, 1, 128).astype(BF16)
    rw_a2p = _pad_axis(rw_a2, 1, 128, before=128 - rw_a2.shape[1]).astype(BF16)
    rw_g2p = _pad_axis(rw_g2, 1, WIDTH).astype(BF16)
    mu = _pad_axis(rw_mu, 1, RW_PAD)[:, None, :]
    cv_vec = _rows8(cv_b, cv_ln_g, cv_ln_b)
    cv_wp = _pad_axis(cv_w, 1, CONV_HALO)
    gla_a2p = _pad_axis(gla_a2, 1, 128).astype(BF16)
    gla_vec = _rows8(gla_ab)
    gla_lng = gla_ln_g[:, None, :]
    bf = _pad_axis(fox_bf, 1, 8)[:, :, None]
    lns = jnp.stack([ln_g, ln_b], axis=2)

    w_small, w_gate, w_fox_t = _split_w_in(w_in)
    ups = [_to_bf16(w) for w in (rw_up, cv_up, gla_up, fox_up)]
    wo_b, xq_b, xk_b, xv_b, xo_b = (_to_bf16(w) for w in (w_out, xa_wq, xa_wk, xa_wv, xa_wo))
    w1_b, w3_b, w2_b = (_to_bf16(w) for w in (ffn_w1, ffn_w3, ffn_w2))

    for l in range(depth):
        x2 = x.reshape(m, d)
        p_rw, p_cv, p_gla, f_k, f_qt, f_vt, f_zt = _in_proj(x2, w_small, w_fox_t, l, tm)
        p_rw = p_rw.reshape(bsz, t, RW_PAD)
        p_cv = p_cv.reshape(bsz, t, CV_COLS)
        p_gla = p_gla.reshape(bsz, t, GLA_PAD)

        y_rw = _rwkv(p_rw, mu, rw_vec, rw_w2p, rw_a2p, rw_g2p, l, nb, CHUNK)
        y_cv = _conv(p_cv, cv_wp, cv_vec, l, tb)
        y_gla = _gla(p_gla, gla_a2p, gla_vec, gla_lng, l, nb, CHUNK)
        c_row = _fox_gate(f_zt, bf, l, t)
        y_fox = _fox(f_qt, f_k, f_vt, c_row, c_row.T, t, tb)

        ys = [y.reshape(m, WIDTH) for y in (y_rw, y_cv, y_gla)] + [y_fox]
        x2 = _merge(x2, ys, w_gate, gate_b, ups, wo_b, lns, l, alpha, tm)

        tkv = min(512, bsz * mlen)
        k_mem = _matmul(mem2, xk_b, l, BF16, tkv).reshape(bsz, mlen, d)
        v_mem = _matmul(mem2, xv_b, l, BF16, tkv).reshape(bsz, mlen, d)
        x3 = _xattn(x2.reshape(bsz, t, d), k_mem, v_mem, xq_b, xo_b, lns, l, alpha, tm)

        x2 = _ffn(x3.reshape(m, d), w1_b, w3_b, w2_b, lns, l, alpha, tm)
        x = x2.reshape(bsz, t, d)
    return x
```

```python
import functools

import jax
import jax.numpy as jnp
from jax import lax
from jax.experimental import pallas as pl
from jax.experimental.pallas import tpu as pltpu

F32 = jnp.float32
BF16 = jnp.bfloat16

D_MODEL = 1024
WIDTH = 256
HEADS = 4
HEAD_DIM = WIDTH // HEADS
CHUNK = 64
RW_COLS = 1056
RW_PAD = 1152
CV_COLS = 512
GLA_COLS = 784
GLA_PAD = 896
GLA_DK = 32
FOX_COLS = 772
FOX_ZROWS = 16
FEAT_SLAB = 128
FOX_T_ROWS = 2 * WIDTH + FEAT_SLAB
SMALL_COLS = RW_PAD + CV_COLS + GLA_PAD + WIDTH
GATE_COLS = 4 * D_MODEL
CONV_WIDTH = 31
CONV_HALO = 32
D_FF = 2816
FF_CHUNK = 1408
RW_LN_EPS = 64e-5
GLA_EPS = 1e-5
GLA_TAU = 16.0
LN_EPS = 1e-5
NEG_BIG = -1e30
LOG2E = 1.4426950408889634
VMEM_LIMIT = 56 * 1024 * 1024


def _bdot(a, b):
    return jnp.dot(a.astype(BF16), b.astype(BF16), preferred_element_type=F32)


def _bdot_nt(a, b):
    return lax.dot_general(a.astype(BF16), b.astype(BF16), (((1,), (1,)), ((), ())),
                           preferred_element_type=F32)


def _bdot_tn(a, b):
    return lax.dot_general(a.astype(BF16), b.astype(BF16), (((0,), (0,)), ((), ())),
                           preferred_element_type=F32)


def _split3(x):
    hi = x.astype(BF16)
    r1 = x - hi.astype(F32)
    mid = r1.astype(BF16)
    lo = (r1 - mid.astype(F32)).astype(BF16)
    return hi, mid, lo


def _dot3_right(x, m):
    return sum(jnp.dot(p, m, preferred_element_type=F32) for p in _split3(x))


def _dot3_left(m, x):
    return sum(jnp.dot(m, p, preferred_element_type=F32) for p in _split3(x))


def _softplus(z):
    return jnp.maximum(z, 0.0) + jnp.log1p(jnp.exp(-jnp.abs(z)))


def _log_sigmoid(z):
    return -_softplus(-z)


def _sigmoid(z):
    return jax.nn.sigmoid(z)


def _layer_norm(z, g, b):
    mu = jnp.mean(z, axis=-1, keepdims=True)
    d = z - mu
    var = jnp.mean(d * d, axis=-1, keepdims=True)
    return d * lax.rsqrt(var + LN_EPS) * g + b


def _head_masks(rows, row_group, cols, col_group):
    ri = lax.broadcasted_iota(jnp.int32, (rows, cols), 0)
    ci = lax.broadcasted_iota(jnp.int32, (rows, cols), 1)
    same = (ri // row_group) == (ci // col_group)
    return same, ri, ci


def _stack4(x):
    return jnp.concatenate([x, x, x, x], axis=0)


def _fold4(x, n):
    return x[0:n] + x[n:2 * n] + x[2 * n:3 * n] + x[3 * n:4 * n]


def _layer_spec(shape, *lead):
    nd = len(shape)
    return pl.BlockSpec((None,) * len(lead) + tuple(shape), lambda *_: tuple(lead) + (0,) * nd,
                        pipeline_mode=pl.Buffered(1))


def _params(sem):
    return pltpu.CompilerParams(dimension_semantics=sem, vmem_limit_bytes=VMEM_LIMIT)


def _mm_kernel(a_ref, w_ref, o_ref):
    o_ref[...] = _bdot(a_ref[...], w_ref[...]).astype(o_ref.dtype)


def _matmul(a, w, l, out_dtype, tm):
    m, k = a.shape
    n = w.shape[2]
    return pl.pallas_call(
        _mm_kernel,
        grid=(m // tm,),
        in_specs=[pl.BlockSpec((tm, k), lambda i: (i, 0)), _layer_spec((k, n), l)],
        out_specs=pl.BlockSpec((tm, n), lambda i: (i, 0)),
        out_shape=jax.ShapeDtypeStruct((m, n), out_dtype),
        compiler_params=_params(("parallel",)),
        name="matmul",
    )(a, w)


def _cast_kernel(w_ref, o_ref):
    o_ref[...] = w_ref[...].astype(o_ref.dtype)


def _to_bf16(w):
    depth, r, c = w.shape
    tr = r if r * c <= 2 * 1024 * 1024 else 256
    return pl.pallas_call(
        _cast_kernel,
        grid=(depth, r // tr),
        in_specs=[pl.BlockSpec((None, tr, c), lambda l, i: (l, i, 0))],
        out_specs=pl.BlockSpec((None, tr, c), lambda l, i: (l, i, 0)),
        out_shape=jax.ShapeDtypeStruct((depth, r, c), BF16),
        compiler_params=_params(("parallel", "parallel")),
        name="cast_bf16",
    )(w)


def _stage_kernel(w_ref, o_ref, *, pieces, transpose):
    j = pl.program_id(0)
    valid = jnp.int32(0)
    first = 0
    for _, n in pieces:
        chunks = -(-n // FEAT_SLAB)
        valid = jnp.where((j >= first) & (j < first + chunks), n - (j - first) * FEAT_SLAB, valid)
        first += chunks
    row = lax.broadcasted_iota(jnp.int32, (FEAT_SLAB, 1), 0)
    for l in range(w_ref.shape[1]):
        blk = jnp.where(row < valid, w_ref[:, l, :], 0.0)
        o_ref[l] = (blk.T if transpose else blk).astype(o_ref.dtype)


def _stage_w_in(w_t, pieces, transpose):
    n_feat, depth, d = w_t.shape
    total = sum(-(-n // FEAT_SLAB) for _, n in pieces)

    def start(j):
        s, first = jnp.int32(0), 0
        for p0, n in pieces:
            chunks = -(-n // FEAT_SLAB)
            s = jnp.where((j >= first) & (j < first + chunks), p0 + (j - first) * FEAT_SLAB, s)
            first += chunks
        return s

    if transpose:
        out_spec = pl.BlockSpec((depth, d, FEAT_SLAB), lambda j: (0, 0, j))
        out_shape = jax.ShapeDtypeStruct((depth, d, total * FEAT_SLAB), BF16)
    else:
        out_spec = pl.BlockSpec((depth, FEAT_SLAB, d), lambda j: (0, j, 0))
        out_shape = jax.ShapeDtypeStruct((depth, total * FEAT_SLAB, d), BF16)
    return pl.pallas_call(
        functools.partial(_stage_kernel, pieces=pieces, transpose=transpose),
        grid=(total,),
        in_specs=[pl.BlockSpec((pl.Element(FEAT_SLAB), pl.Element(depth), pl.Element(d)),
                               lambda j: (start(j), 0, 0))],
        out_specs=out_spec,
        out_shape=out_shape,
        compiler_params=_params(("parallel",)),
        name="stage_w_in",
    )(w_t)


def _proj_kernel(x_ref, w_ref, wt_ref, rw_ref, cv_ref, gla_ref, fk_ref, fqt_ref, fvt_ref, fzt_ref):
    xb = x_ref[...].astype(BF16)
    off = 0
    for ref, n in ((rw_ref, RW_PAD), (cv_ref, CV_COLS), (gla_ref, GLA_PAD), (fk_ref, WIDTH)):
        ref[...] = jnp.dot(xb, w_ref[:, off:off + n], preferred_element_type=F32)
        off += n
    fqt_ref[...] = _bdot_nt(wt_ref[0:WIDTH, :], xb)
    fvt_ref[...] = _bdot_nt(wt_ref[WIDTH:2 * WIDTH, :], xb)
    fzt_ref[...] = _bdot_nt(wt_ref[2 * WIDTH:2 * WIDTH + FOX_ZROWS, :], xb)[0:8]


def _in_proj(x2, w_small, w_fox_t, l, tm):
    m = x2.shape[0]
    widths = (RW_PAD, CV_COLS, GLA_PAD, WIDTH)
    heights = (WIDTH, WIDTH, 8)
    return pl.pallas_call(
        _proj_kernel,
        grid=(m // tm,),
        in_specs=[pl.BlockSpec((tm, D_MODEL), lambda i: (i, 0)),
                  _layer_spec((D_MODEL, SMALL_COLS), l), _layer_spec((FOX_T_ROWS, D_MODEL), l)],
        out_specs=[pl.BlockSpec((tm, n), lambda i: (i, 0)) for n in widths]
        + [pl.BlockSpec((r, tm), lambda i: (0, i)) for r in heights],
        out_shape=[jax.ShapeDtypeStruct((m, n), F32) for n in widths]
        + [jax.ShapeDtypeStruct((r, m), F32) for r in heights],
        compiler_params=_params(("parallel",)),
        name="in_proj",
    )(x2, w_small, w_fox_t)


def _rwkv_kernel(p_ref, mu_ref, vec_ref, w2_ref, a2_ref, g2_ref, o_ref, st_ref, prev_ref, *, nb, tb):
    c = WIDTH
    n = CHUNK
    seqs = range(nb)

    @pl.when(pl.program_id(1) == 0)
    def _():
        st_ref[...] = jnp.zeros_like(st_ref)
        prev_ref[...] = jnp.zeros_like(prev_ref)

    row = lax.broadcasted_iota(jnp.int32, (tb, 1), 0)
    shifted = []
    for b in seqs:
        p = p_ref[b]
        prev = jnp.where(row == 0, prev_ref[b], pltpu.roll(p, 1, axis=0))
        prev_ref[b] = p[tb - 1:tb, :]
        shifted.append(p + (prev - p) * mu_ref[...])
    ps = jnp.concatenate(shifted, axis=0)

    w0, a0, k_k, k_a = vec_ref[0:1], vec_ref[1:2], vec_ref[2:3], vec_ref[3:4]
    r_k, ln_g, ln_b = vec_ref[4:5], vec_ref[5:6], vec_ref[6:7]

    r = ps[:, 0:c]
    k = ps[:, c:2 * c]
    v = ps[:, 2 * c:3 * c]
    wa = ps[:, 3 * c:3 * c + 128]
    xg = ps[:, 3 * c + 128:3 * c + 384]

    w_log = -_softplus(-(w0 + _bdot(jnp.tanh(wa), w2_ref[...]))) - 0.5
    logw = -jnp.exp(w_log)
    a_sig = _sigmoid(a0 + _bdot(wa, a2_ref[...]))
    g = _bdot(_sigmoid(xg), g2_ref[...])

    same, ri, ci = _head_masks(c, n, c, n)
    tr, tc = ri % n, ci % n
    same_f = jnp.where(same, 1.0, 0.0)
    ones_bd = same_f.astype(BF16)
    strict_bd = jnp.where(same & (tr > tc), 1.0, 0.0).astype(BF16)
    incl_bd = jnp.where(same & (tr >= tc), 1.0, 0.0).astype(BF16)
    eye = jnp.where(same & (tr == tc), 1.0, 0.0)
    tril = incl_bd[0:n, 0:n]
    sc_mask = jnp.concatenate([jnp.concatenate([strict_bd, strict_bd], axis=1),
                               jnp.concatenate([incl_bd, incl_bd], axis=1)], axis=0)

    kk = k * k_k
    kk = kk / jnp.maximum(jnp.sqrt(_dot3_right(kk * kk, ones_bd)), 1e-12)
    k_mod = k * (1.0 + (a_sig - 1.0) * k_a)
    a_vec = -kk
    b_vec = kk * a_sig

    mm = lambda x, y: jnp.dot(x, y, preferred_element_type=F32)
    rep = lambda x: _stack4(x.astype(BF16))
    bd = lambda x: rep(x) * ones_bd

    ys = [[] for _ in seqs]
    for ch in range(tb // n):
        sls = [slice(b * tb + ch * n, b * tb + (ch + 1) * n) for b in seqs]
        lw = [logw[s] for s in sls]
        cum = [_dot3_left(tril, x) for x in lw]
        last = [x[n - 1:n, :] for x in cum]
        g_inv = [jnp.exp(-x) for x in cum]
        g_end = [jnp.exp(l - x) for l, x in zip(last, cum)]
        at = [a_vec[s] * jnp.exp(x - w) for s, x, w in zip(sls, cum, lw)]
        rt = [r[s] * jnp.exp(x) for s, x in zip(sls, cum)]
        bt = [b_vec[s] * gi for s, gi in zip(sls, g_inv)]
        kt = [k_mod[s] * gi for s, gi in zip(sls, g_inv)]
        bh = [b_vec[s] * ge for s, ge in zip(sls, g_end)]
        kh = [k_mod[s] * ge for s, ge in zip(sls, g_end)]

        lhs = [jnp.concatenate([bd(x), bd(y)], axis=0) for x, y in zip(at, rt)]
        rhs = [jnp.concatenate([rep(x), rep(y)], axis=0) for x, y in zip(bt, kt)]
        sc = [_bdot_nt(x, y).astype(BF16) * sc_mask for x, y in zip(lhs, rhs)]
        m_ab = [x[0:c, 0:c] for x in sc]
        m_ak = [x[0:c, c:2 * c] for x in sc]
        m_rb = [x[c:2 * c, 0:c] for x in sc]
        m_rk = [x[c:2 * c, c:2 * c] for x in sc]

        t_inv = [eye + x.astype(F32) for x in m_ab]
        m_pow = m_ab
        for _ in range(5):
            m_pow = [mm(x, x).astype(BF16) for x in m_pow]
            t_inv = [t + mm(x, t.astype(BF16)) for x, t in zip(m_pow, t_inv)]
        t_inv = [t.astype(BF16) for t in t_inv]

        bdv = [bd(v[s]) for s in sls]
        pk = [mm(x, y) for x, y in zip(m_ak, bdv)]
        yk = [mm(x, y) for x, y in zip(m_rk, bdv)]
        st = [st_ref[b] for b in seqs]
        ws = [_bdot_nt(x, s) for x, s in zip(lhs, st)]
        u_bd = [mm(t, (w[0:c] + p_).astype(BF16)) for t, w, p_ in zip(t_inv, ws, pk)]
        y_bd = [w[c:2 * c] + mm(x, u.astype(BF16)) + z for w, x, u, z in zip(ws, m_rb, u_bd, yk)]
        upd = [_bdot_tn(jnp.concatenate([_fold4(u, n), v[s]], axis=0), jnp.concatenate([x, y], axis=0))
               for u, s, x, y in zip(u_bd, sls, bh, kh)]
        for b in seqs:
            ys[b].append(_fold4(y_bd[b], n))
            st_ref[b] = st[b] * jnp.exp(last[b]) + upd[b] * same_f

    y = jnp.concatenate([blk for b in seqs for blk in ys[b]], axis=0)
    inv_n = 1.0 / HEAD_DIM
    mu_y = _dot3_right(y, ones_bd) * inv_n
    dy = y - mu_y
    var_y = _dot3_right(dy * dy, ones_bd) * inv_n
    yn = dy * lax.rsqrt(var_y + RW_LN_EPS) * ln_g + ln_b
    bonus = _dot3_right(r * k_mod * r_k, ones_bd) * v
    out = (yn + bonus) * g
    for b in seqs:
        o_ref[b] = out[b * tb:(b + 1) * tb]


def _rwkv(p_rw, mu, vec, w2, a2, g2, l, nb, tb):
    bsz, t, _ = p_rw.shape
    return pl.pallas_call(
        functools.partial(_rwkv_kernel, nb=nb, tb=tb),
        grid=(bsz // nb, t // tb),
        in_specs=[pl.BlockSpec((nb, tb, RW_PAD), lambda b, i: (b, i, 0)),
                  _layer_spec((1, RW_PAD), l), _layer_spec((8, WIDTH), l),
                  _layer_spec((128, WIDTH), l), _layer_spec((128, WIDTH), l),
                  _layer_spec((WIDTH, WIDTH), l)],
        out_specs=pl.BlockSpec((nb, tb, WIDTH), lambda b, i: (b, i, 0)),
        out_shape=jax.ShapeDtypeStruct((bsz, t, WIDTH), F32),
        scratch_shapes=[pltpu.VMEM((nb, WIDTH, WIDTH), F32), pltpu.VMEM((nb, 1, RW_PAD), F32)],
        compiler_params=_params(("parallel", "arbitrary")),
        name="rwkv7",
    )(p_rw, mu, vec, w2, a2, g2)


def _conv_kernel(p_ref, w_ref, vec_ref, o_ref, buf_ref, sh_ref, *, tb):
    @pl.when(pl.program_id(1) == 0)
    def _():
        buf_ref[0:CONV_HALO, :] = jnp.zeros((CONV_HALO, WIDTH), F32)

    p = p_ref[0]
    buf_ref[CONV_HALO:CONV_HALO + tb, :] = p[:, 0:WIDTH] * _sigmoid(p[:, WIDTH:2 * WIDTH])
    acc = jnp.zeros((tb, WIDTH), F32) + vec_ref[0:1]
    first = CONV_HALO - (CONV_WIDTH - 1)
    for s in range(8):
        last_a = (CONV_WIDTH - 1 - s) // 8
        span = tb + 8 * last_a
        sh_ref[0:span, :] = buf_ref[first + s:first + s + span, :]
        for a in range(last_a + 1):
            kx = 8 * a + s
            acc = acc + sh_ref[8 * a:8 * a + tb, :] * w_ref[kx:kx + 1, :]
    buf_ref[0:CONV_HALO, :] = buf_ref[tb:tb + CONV_HALO, :]
    u = _layer_norm(acc, vec_ref[1:2], vec_ref[2:3])
    o_ref[0] = u * _sigmoid(u)


def _conv(p_cv, w, vec, l, tb):
    bsz, t, _ = p_cv.shape
    return pl.pallas_call(
        functools.partial(_conv_kernel, tb=tb),
        grid=(bsz, t // tb),
        in_specs=[pl.BlockSpec((1, tb, CV_COLS), lambda b, i: (b, i, 0)),
                  _layer_spec((CONV_HALO, WIDTH), l), _layer_spec((8, WIDTH), l)],
        out_specs=pl.BlockSpec((1, tb, WIDTH), lambda b, i: (b, i, 0)),
        out_shape=jax.ShapeDtypeStruct((bsz, t, WIDTH), F32),
        scratch_shapes=[pltpu.VMEM((tb + CONV_HALO, WIDTH), F32), pltpu.VMEM((tb + CONV_HALO, WIDTH), F32)],
        compiler_params=_params(("parallel", "arbitrary")),
        name="conv",
    )(p_cv, w, vec)


def _gla_kernel(p_ref, a2_ref, vec_ref, lng_ref, o_ref, st_ref, *, nb, tb):
    c = WIDTH
    n = CHUNK
    dk = HEADS * GLA_DK

    seqs = range(nb)

    @pl.when(pl.program_id(1) == 0)
    def _():
        st_ref[...] = jnp.zeros_like(st_ref)

    p = jnp.concatenate([p_ref[b] for b in seqs], axis=0)
    q = p[:, 0:dk] * (GLA_DK ** -0.5)
    k = p[:, dk:2 * dk]
    v = p[:, 2 * dk:2 * dk + c]
    gr = p[:, 2 * dk + c:2 * dk + 2 * c]
    z = p[:, 2 * dk + 2 * c:2 * dk + 2 * c + 128]
    log_a = _log_sigmoid(_bdot(z, a2_ref[...]) + vec_ref[0:1]) / GLA_TAU

    same_q, _, _ = _head_masks(c, n, dk, GLA_DK)
    same, ri, ci = _head_masks(c, n, c, n)
    same_qf = jnp.where(same_q, 1.0, 0.0)
    same_qb = same_qf.astype(BF16)
    ones_bd = jnp.where(same, 1.0, 0.0).astype(BF16)
    incl_bd = jnp.where(same & ((ri % n) >= (ci % n)), 1.0, 0.0).astype(BF16)
    tril = incl_bd[0:n, 0:n]

    mm = lambda x, y: jnp.dot(x, y, preferred_element_type=F32)
    rep = lambda x: _stack4(x.astype(BF16))

    os_ = [[] for _ in seqs]
    for ch in range(tb // n):
        sls = [slice(b * tb + ch * n, b * tb + (ch + 1) * n) for b in seqs]
        cum = [_dot3_left(tril, log_a[s]) for s in sls]
        last = [x[n - 1:n, :] for x in cum]
        q_bd = [rep(q[s] * jnp.exp(x)) * same_qb for s, x in zip(sls, cum)]
        k_inv = [rep(k[s] * jnp.exp(-x)) for s, x in zip(sls, cum)]
        k_end = [k[s] * jnp.exp(l - x) for s, l, x in zip(sls, last, cum)]
        sc = [jnp.where(incl_bd > 0, _bdot_nt(x, y).astype(BF16), 0.0) for x, y in zip(q_bd, k_inv)]
        st = [st_ref[b] for b in seqs]
        o_bd = [mm(x, rep(v[s]) * ones_bd) + _bdot_nt(y, z) for x, s, y, z in zip(sc, sls, q_bd, st)]
        upd = [_bdot_tn(v[s], x) for s, x in zip(sls, k_end)]
        for b in seqs:
            os_[b].append(_fold4(o_bd[b], n))
            st_ref[b] = st[b] * jnp.exp(last[b]) + upd[b] * same_qf

    o = jnp.concatenate([blk for b in seqs for blk in os_[b]], axis=0)
    ms = _dot3_right(o * o, ones_bd) * (1.0 / HEAD_DIM)
    o = o * lax.rsqrt(ms + GLA_EPS) * lng_ref[...]
    out = o * (gr * _sigmoid(gr))
    for b in seqs:
        o_ref[b] = out[b * tb:(b + 1) * tb]


def _gla(p_gla, a2, vec, ln_g, l, nb, tb):
    bsz, t, _ = p_gla.shape
    return pl.pallas_call(
        functools.partial(_gla_kernel, nb=nb, tb=tb),
        grid=(bsz // nb, t // tb),
        in_specs=[pl.BlockSpec((nb, tb, GLA_PAD), lambda b, i: (b, i, 0)),
                  _layer_spec((128, 128), l), _layer_spec((8, 128), l), _layer_spec((1, WIDTH), l)],
        out_specs=pl.BlockSpec((nb, tb, WIDTH), lambda b, i: (b, i, 0)),
        out_shape=jax.ShapeDtypeStruct((bsz, t, WIDTH), F32),
        scratch_shapes=[pltpu.VMEM((nb, WIDTH, HEADS * GLA_DK), F32)],
        compiler_params=_params(("parallel", "arbitrary")),
        name="gla",
    )(p_gla, a2, vec, ln_g)


def _fox_gate_kernel(z_ref, bf_ref, c_ref):
    x = _log_sigmoid(z_ref[...] + bf_ref[...])
    t = x.shape[1]
    lane = lax.broadcasted_iota(jnp.int32, x.shape, 1)
    sh = 1
    while sh < t:
        x = x + jnp.where(lane >= sh, pltpu.roll(x, sh, axis=1), 0.0)
        sh *= 2
    c_ref[...] = x


def _fox_gate(z_rows, bf, l, t):
    hp, m = z_rows.shape
    return pl.pallas_call(
        _fox_gate_kernel,
        grid=(m // t,),
        in_specs=[pl.BlockSpec((hp, t), lambda b: (0, b)), _layer_spec((hp, 1), l)],
        out_specs=pl.BlockSpec((hp, t), lambda b: (0, b)),
        out_shape=jax.ShapeDtypeStruct((hp, m), F32),
        compiler_params=_params(("parallel",)),
        name="fox_gate",
    )(z_rows, bf)


def _fox_kernel(qt_ref, k_ref, vt_ref, crow_ref, ccol_ref, o_ref, *, tq):
    c = WIDTH
    i = pl.program_id(1)
    qt = qt_ref[...] * (HEAD_DIM ** -0.5 * LOG2E)
    row_head = lax.broadcasted_iota(jnp.int32, (c, 1), 0) // HEAD_DIM
    qtm = jnp.concatenate([jnp.where(row_head == h, qt, 0.0) for h in range(HEADS)], axis=1).astype(BF16)
    cq = [crow_ref[h:h + 1, :] * LOG2E for h in range(HEADS)]
    causal = (lax.broadcasted_iota(jnp.int32, (tq, tq), 0)
              <= lax.broadcasted_iota(jnp.int32, (tq, tq), 1))

    def step(ks, tk, carry, masked):
        kblk = k_ref[pl.ds(ks, tk), :].astype(BF16)
        s_all = jnp.dot(kblk, qtm, preferred_element_type=F32)
        stats = []
        for h in range(HEADS):
            m, l, _ = carry[h]
            ck = ccol_ref[pl.ds(ks, tk), h:h + 1] * LOG2E
            s = s_all[:, h * tq:(h + 1) * tq] - ck
            if masked:
                s = jnp.where(causal, s, NEG_BIG)
            m_new = jnp.maximum(m, jnp.max(s, axis=0, keepdims=True) + cq[h])
            alpha = jnp.exp2(m - m_new)
            pr = jnp.exp2(s - (m_new - cq[h]))
            l = alpha * l + jnp.sum(pr, axis=0, keepdims=True)
            stats.append((m_new, l, alpha, pr.astype(BF16)))
        new = []
        for h in range(HEADS):
            m_new, l, alpha, pr = stats[h]
            vt_h = vt_ref[h * HEAD_DIM:(h + 1) * HEAD_DIM, pl.ds(ks, tk)].astype(BF16)
            acc = alpha * carry[h][2] + jnp.dot(vt_h, pr, preferred_element_type=F32)
            new.append((m_new, l, acc))
        return tuple(new)

    init = tuple((jnp.full((1, tq), NEG_BIG, F32), jnp.zeros((1, tq), F32),
                  jnp.zeros((HEAD_DIM, tq), F32)) for _ in range(HEADS))
    pairs = i // 2
    carry = lax.fori_loop(
        0, pairs, lambda kb, cr: step(pl.multiple_of(kb * 2 * tq, 2 * tq), 2 * tq, cr, False), init)
    carry = lax.fori_loop(
        2 * pairs, i, lambda kb, cr: step(pl.multiple_of(kb * tq, tq), tq, cr, False), carry)
    carry = step(pl.multiple_of(i * tq, tq), tq, carry, True)
    out_t = jnp.concatenate([acc / l for _, l, acc in carry], axis=0)
    o_ref[...] = out_t.T


def _fox(q_t, k, v_t, c_row, c_col, t, tq):
    m = k.shape[0]
    nq = t // tq
    hp = c_row.shape[0]
    return pl.pallas_call(
        functools.partial(_fox_kernel, tq=tq),
        grid=(m // t, nq),
        in_specs=[pl.BlockSpec((WIDTH, tq), lambda b, i: (0, b * nq + i)),
                  pl.BlockSpec((t, WIDTH), lambda b, i: (b, 0)),
                  pl.BlockSpec((WIDTH, t), lambda b, i: (0, b)),
                  pl.BlockSpec((hp, tq), lambda b, i: (0, b * nq + i)),
                  pl.BlockSpec((t, hp), lambda b, i: (b, 0))],
        out_specs=pl.BlockSpec((tq, WIDTH), lambda b, i: (b * nq + i, 0)),
        out_shape=jax.ShapeDtypeStruct((m, WIDTH), F32),
        compiler_params=_params(("parallel", "arbitrary")),
        name="fox",
    )(q_t, k, v_t, c_row, c_col)


def _merge_kernel(x_ref, y0_ref, y1_ref, y2_ref, y3_ref, wg_ref, gb_ref, u0_ref, u1_ref, u2_ref, u3_ref,
                  wo_ref, ln_ref, o_ref, *, alpha):
    x = x_ref[...]
    xb = x.astype(BF16)
    merged = None
    branches = ((y0_ref, u0_ref), (y1_ref, u1_ref), (y2_ref, u2_ref), (y3_ref, u3_ref))
    for j, (y_ref, up_ref) in enumerate(branches):
        gate = _sigmoid(jnp.dot(xb, wg_ref[:, j * D_MODEL:(j + 1) * D_MODEL],
                                preferred_element_type=F32) + gb_ref[j:j + 1, :])
        term = gate * _bdot(y_ref[...], up_ref[...])
        merged = term if merged is None else merged + term
    z = alpha * x + _bdot(merged, wo_ref[...])
    o_ref[...] = _layer_norm(z, ln_ref[0:1], ln_ref[1:2])


def _merge(x2, ys, wg, gb, ups, wo, ln, l, alpha, tm):
    m = x2.shape[0]
    row = lambda n: pl.BlockSpec((tm, n), lambda i: (i, 0))
    return pl.pallas_call(
        functools.partial(_merge_kernel, alpha=alpha),
        grid=(m // tm,),
        in_specs=[row(D_MODEL)] + [row(WIDTH)] * 4 + [
            _layer_spec((D_MODEL, GATE_COLS), l), _layer_spec((4, D_MODEL), l)]
        + [_layer_spec((WIDTH, D_MODEL), l)] * 4 + [
            _layer_spec((D_MODEL, D_MODEL), l), _layer_spec((2, D_MODEL), l, 0)],
        out_specs=row(D_MODEL),
        out_shape=jax.ShapeDtypeStruct((m, D_MODEL), F32),
        compiler_params=_params(("parallel",)),
        name="merge",
    )(x2, *ys, wg, gb, *ups, wo, ln)


def _xattn_kernel(x_ref, k_ref, v_ref, wq_ref, wo_ref, ln_ref, o_ref, *, alpha):
    x = x_ref[0]
    q = _bdot(x, wq_ref[...])
    hd = D_MODEL // HEADS
    outs = []
    for h in range(HEADS):
        sl = slice(h * hd, (h + 1) * hd)
        s = _bdot_nt(q[:, sl], k_ref[0, :, sl]) * (hd ** -0.5)
        e = jnp.exp(s - jnp.max(s, axis=-1, keepdims=True))
        pr = e / jnp.sum(e, axis=-1, keepdims=True)
        outs.append(_bdot(pr, v_ref[0, :, sl]))
    y = _bdot(jnp.concatenate(outs, axis=1), wo_ref[...])
    o_ref[0] = _layer_norm(alpha * x + y, ln_ref[0:1], ln_ref[1:2])


def _xattn(x, k, v, wq, wo, ln, l, alpha, tm):
    bsz, t, _ = x.shape
    mlen = k.shape[1]
    return pl.pallas_call(
        functools.partial(_xattn_kernel, alpha=alpha),
        grid=(bsz, t // tm),
        in_specs=[pl.BlockSpec((1, tm, D_MODEL), lambda b, i: (b, i, 0)),
                  pl.BlockSpec((1, mlen, D_MODEL), lambda b, i: (b, 0, 0)),
                  pl.BlockSpec((1, mlen, D_MODEL), lambda b, i: (b, 0, 0)),
                  _layer_spec((D_MODEL, D_MODEL), l), _layer_spec((D_MODEL, D_MODEL), l),
                  _layer_spec((2, D_MODEL), l, 1)],
        out_specs=pl.BlockSpec((1, tm, D_MODEL), lambda b, i: (b, i, 0)),
        out_shape=jax.ShapeDtypeStruct((bsz, t, D_MODEL), F32),
        compiler_params=_params(("parallel", "parallel")),
        name="xattn",
    )(x, k, v, wq, wo, ln)


def _ffn_kernel(x_ref, w1_ref, w3_ref, w2_ref, ln_ref, o_ref, *, alpha):
    x = x_ref[...]
    xb = x.astype(BF16)
    y = None
    for c0 in range(0, D_FF, FF_CHUNK):
        h1 = jnp.dot(xb, w1_ref[:, c0:c0 + FF_CHUNK], preferred_element_type=F32)
        h3 = jnp.dot(xb, w3_ref[:, c0:c0 + FF_CHUNK], preferred_element_type=F32)
        part = _bdot(h1 * _sigmoid(h1) * h3, w2_ref[c0:c0 + FF_CHUNK, :])
        y = part if y is None else y + part
    o_ref[...] = _layer_norm(alpha * x + y, ln_ref[0:1], ln_ref[1:2])


def _ffn(x2, w1, w3, w2, ln, l, alpha, tm):
    m = x2.shape[0]
    return pl.pallas_call(
        functools.partial(_ffn_kernel, alpha=alpha),
        grid=(m // tm,),
        in_specs=[pl.BlockSpec((tm, D_MODEL), lambda i: (i, 0)),
                  _layer_spec((D_MODEL, D_FF), l), _layer_spec((D_MODEL, D_FF), l),
                  _layer_spec((D_FF, D_MODEL), l), _layer_spec((2, D_MODEL), l, 2)],
        out_specs=pl.BlockSpec((tm, D_MODEL), lambda i: (i, 0)),
        out_shape=jax.ShapeDtypeStruct((m, D_MODEL), F32),
        compiler_params=_params(("parallel",)),
        name="ffn",
    )(x2, w1, w3, w2, ln)


def _pad_axis(a, axis, n, before=0):
    widths = [(0, 0)] * a.ndim
    widths[axis] = (before, n - a.shape[axis] - before)
    return jnp.pad(a, widths)


def _rows8(*vs):
    depth = vs[0].shape[0]
    return _pad_axis(jnp.stack([v.reshape(depth, -1) for v in vs], axis=1), 1, 8)


def kernel(x, mem, w_in, rw_mu, rw_w0, rw_w2, rw_a0, rw_a2, rw_g2, rw_kk, rw_ka, rw_rk, rw_ln_g, rw_ln_b, rw_up, cv_w, cv_b, cv_ln_g, cv_ln_b, cv_up, gla_a2, gla_ab, gla_ln_g, gla_up, fox_bf, fox_up, gate_b, w_out, xa_wq, xa_wk, xa_wv, xa_wo, ffn_w1, ffn_w3, ffn_w2, ln_g, ln_b):
    bsz, t, d = x.shape
    depth = w_in.shape[0]
    alpha = (2.0 * depth) ** 0.25
    m = bsz * t
    mlen = mem.shape[1]
    tm = min(512, t)
    tb = min(256, t)
    nb = max(g for g in (8, 4, 2, 1) if bsz % g == 0)
    mem2 = mem.reshape(bsz * mlen, d)

    rw_vec = _rows8(rw_w0, rw_a0, rw_kk, rw_ka, rw_rk, rw_ln_g, rw_ln_b)
    rw_w2p = _pad_axis(rw_w2, 1, 128).astype(BF16)
    rw_a2p = _pad_axis(rw_a2, 1, 128, before=128 - rw_a2.shape[1]).astype(BF16)
    rw_g2p = _pad_axis(rw_g2, 1, WIDTH).astype(BF16)
    mu = _pad_axis(rw_mu, 1, RW_PAD)[:, None, :]
    cv_vec = _rows8(cv_b, cv_ln_g, cv_ln_b)
    cv_wp = _pad_axis(cv_w, 1, CONV_HALO)
    gla_a2p = _pad_axis(gla_a2, 1, 128).astype(BF16)
    gla_vec = _rows8(gla_ab)
    gla_lng = gla_ln_g[:, None, :]
    bf = _pad_axis(fox_bf, 1, 8)[:, :, None]
    lns = jnp.stack([ln_g, ln_b], axis=2)

    o_cv = RW_COLS
    o_gla = o_cv + CV_COLS
    o_fq = o_gla + GLA_COLS
    o_fk, o_fv, o_fz = o_fq + WIDTH, o_fq + 2 * WIDTH, o_fq + 3 * WIDTH
    o_gate = o_fq + FOX_COLS
    w_t = jnp.transpose(w_in, (2, 0, 1))
    w_small = _stage_w_in(w_t, ((0, RW_COLS), (o_cv, CV_COLS), (o_gla, GLA_COLS), (o_fk, WIDTH)), True)
    w_gate = _stage_w_in(w_t, ((o_gate, GATE_COLS),), True)
    w_fox_t = _stage_w_in(w_t, ((o_fq, WIDTH), (o_fv, WIDTH), (o_fz, HEADS)), False)
    ups = [_to_bf16(w) for w in (rw_up, cv_up, gla_up, fox_up)]
    wo_b, xq_b, xk_b, xv_b, xo_b = (_to_bf16(w) for w in (w_out, xa_wq, xa_wk, xa_wv, xa_wo))
    w1_b, w3_b, w2_b = (_to_bf16(w) for w in (ffn_w1, ffn_w3, ffn_w2))

    for l in range(depth):
        x2 = x.reshape(m, d)
        p_rw, p_cv, p_gla, f_k, f_qt, f_vt, f_zt = _in_proj(x2, w_small, w_fox_t, l, tm)
        p_rw = p_rw.reshape(bsz, t, RW_PAD)
        p_cv = p_cv.reshape(bsz, t, CV_COLS)
        p_gla = p_gla.reshape(bsz, t, GLA_PAD)

        y_rw = _rwkv(p_rw, mu, rw_vec, rw_w2p, rw_a2p, rw_g2p, l, nb, CHUNK)
        y_cv = _conv(p_cv, cv_wp, cv_vec, l, tb)
        y_gla = _gla(p_gla, gla_a2p, gla_vec, gla_lng, l, nb, CHUNK)
        c_row = _fox_gate(f_zt, bf, l, t)
        y_fox = _fox(f_qt, f_k, f_vt, c_row, c_row.T, t, tb)

        ys = [y.reshape(m, WIDTH) for y in (y_rw, y_cv, y_gla)] + [y_fox]
        x2 = _merge(x2, ys, w_gate, gate_b, ups, wo_b, lns, l, alpha, tm)

        tkv = min(512, bsz * mlen)
        k_mem = _matmul(mem2, xk_b, l, BF16, tkv).reshape(bsz, mlen, d)
        v_mem = _matmul(mem2, xv_b, l, BF16, tkv).reshape(bsz, mlen, d)
        x3 = _xattn(x2.reshape(bsz, t, d), k_mem, v_mem, xq_b, xo_b, lns, l, alpha, tm)

        x2 = _ffn(x3.reshape(m, d), w1_b, w3_b, w2_b, lns, l, alpha, tm)
        x = x2.reshape(bsz, t, d)
    return x
```

```python
import functools

import jax
import jax.numpy as jnp
from jax import lax
from jax.experimental import pallas as pl
from jax.experimental.pallas import tpu as pltpu

F32 = jnp.float32
BF16 = jnp.bfloat16

D_MODEL = 1024
WIDTH = 256
HEADS = 4
HEAD_DIM = WIDTH // HEADS
CHUNK = 64
RW_COLS = 1056
RW_PAD = 1152
CV_COLS = 512
GLA_COLS = 784
GLA_PAD = 896
GLA_DK = 32
FOX_COLS = 772
FOX_ZROWS = 16
FEAT_SLAB = 128
FOX_T_ROWS = 2 * WIDTH + FEAT_SLAB
SMALL_COLS = RW_PAD + CV_COLS + GLA_PAD + WIDTH
GATE_COLS = 4 * D_MODEL
CONV_WIDTH = 31
CONV_HALO = 32
D_FF = 2816
FF_CHUNKS = ((0, 1280), (1280, 1536))
RW_LN_EPS = 64e-5
GLA_EPS = 1e-5
GLA_TAU = 16.0
LN_EPS = 1e-5
NEG_BIG = -1e30
LOG2E = 1.4426950408889634
VMEM_LIMIT = 56 * 1024 * 1024


def _bdot(a, b):
    return jnp.dot(a.astype(BF16), b.astype(BF16), preferred_element_type=F32)


def _bdot_nt(a, b):
    return lax.dot_general(a.astype(BF16), b.astype(BF16), (((1,), (1,)), ((), ())),
                           preferred_element_type=F32)


def _bdot_tn(a, b):
    return lax.dot_general(a.astype(BF16), b.astype(BF16), (((0,), (0,)), ((), ())),
                           preferred_element_type=F32)


def _split3(x):
    hi = x.astype(BF16)
    r1 = x - hi.astype(F32)
    mid = r1.astype(BF16)
    lo = (r1 - mid.astype(F32)).astype(BF16)
    return hi, mid, lo


def _dot3_right(x, m):
    return sum(jnp.dot(p, m, preferred_element_type=F32) for p in _split3(x))


def _dot3_left(m, x):
    return sum(jnp.dot(m, p, preferred_element_type=F32) for p in _split3(x))


def _softplus(z):
    return jnp.maximum(z, 0.0) + jnp.log1p(jnp.exp(-jnp.abs(z)))


def _log_sigmoid(z):
    return -_softplus(-z)


def _sigmoid(z):
    return jax.nn.sigmoid(z)


def _layer_norm(z, g, b):
    mu = jnp.mean(z, axis=-1, keepdims=True)
    d = z - mu
    var = jnp.mean(d * d, axis=-1, keepdims=True)
    return d * lax.rsqrt(var + LN_EPS) * g + b


def _head_masks(rows, row_group, cols, col_group):
    ri = lax.broadcasted_iota(jnp.int32, (rows, cols), 0)
    ci = lax.broadcasted_iota(jnp.int32, (rows, cols), 1)
    same = (ri // row_group) == (ci // col_group)
    return same, ri, ci


def _stack4(x):
    return jnp.concatenate([x, x, x, x], axis=0)


def _fold4(x, n):
    return x[0:n] + x[n:2 * n] + x[2 * n:3 * n] + x[3 * n:4 * n]


def _layer_spec(shape, *lead):
    nd = len(shape)
    return pl.BlockSpec((None,) * len(lead) + tuple(shape), lambda *_: tuple(lead) + (0,) * nd,
                        pipeline_mode=pl.Buffered(1))


def _params(sem):
    return pltpu.CompilerParams(dimension_semantics=sem, vmem_limit_bytes=VMEM_LIMIT)


def _mm_kernel(a_ref, w_ref, o_ref):
    o_ref[...] = _bdot(a_ref[...], w_ref[...]).astype(o_ref.dtype)


def _matmul_layers(a, w, out_dtype):
    m, k = a.shape
    depth, _, n = w.shape
    return pl.pallas_call(
        _mm_kernel,
        grid=(depth,),
        in_specs=[pl.BlockSpec((m, k), lambda l: (0, 0)), pl.BlockSpec((None, k, n), lambda l: (l, 0, 0))],
        out_specs=pl.BlockSpec((None, m, n), lambda l: (l, 0, 0)),
        out_shape=jax.ShapeDtypeStruct((depth, m, n), out_dtype),
        compiler_params=_params(("parallel",)),
        name="matmul",
    )(a, w)


def _cast_kernel(w_ref, o_ref):
    o_ref[...] = w_ref[...].astype(o_ref.dtype)


def _to_bf16(w):
    depth, r, c = w.shape
    tr = r if r * c <= 2 * 1024 * 1024 else 256
    return pl.pallas_call(
        _cast_kernel,
        grid=(depth, r // tr),
        in_specs=[pl.BlockSpec((None, tr, c), lambda l, i: (l, i, 0))],
        out_specs=pl.BlockSpec((None, tr, c), lambda l, i: (l, i, 0)),
        out_shape=jax.ShapeDtypeStruct((depth, r, c), BF16),
        compiler_params=_params(("parallel", "parallel")),
        name="cast_bf16",
    )(w)


def _stage_kernel(w_ref, o_ref, *, pieces, transpose):
    j = pl.program_id(0)
    valid = jnp.int32(0)
    first = 0
    for _, n in pieces:
        chunks = -(-n // FEAT_SLAB)
        valid = jnp.where((j >= first) & (j < first + chunks), n - (j - first) * FEAT_SLAB, valid)
        first += chunks
    row = lax.broadcasted_iota(jnp.int32, (FEAT_SLAB, 1), 0)
    for l in range(w_ref.shape[1]):
        blk = jnp.where(row < valid, w_ref[:, l, :], 0.0)
        o_ref[l] = (blk.T if transpose else blk).astype(o_ref.dtype)


def _stage_w_in(w_t, pieces, transpose):
    n_feat, depth, d = w_t.shape
    total = sum(-(-n // FEAT_SLAB) for _, n in pieces)

    def start(j):
        s, first = jnp.int32(0), 0
        for p0, n in pieces:
            chunks = -(-n // FEAT_SLAB)
            s = jnp.where((j >= first) & (j < first + chunks), p0 + (j - first) * FEAT_SLAB, s)
            first += chunks
        return s

    if transpose:
        out_spec = pl.BlockSpec((depth, d, FEAT_SLAB), lambda j: (0, 0, j))
        out_shape = jax.ShapeDtypeStruct((depth, d, total * FEAT_SLAB), BF16)
    else:
        out_spec = pl.BlockSpec((depth, FEAT_SLAB, d), lambda j: (0, j, 0))
        out_shape = jax.ShapeDtypeStruct((depth, total * FEAT_SLAB, d), BF16)
    return pl.pallas_call(
        functools.partial(_stage_kernel, pieces=pieces, transpose=transpose),
        grid=(total,),
        in_specs=[pl.BlockSpec((pl.Element(FEAT_SLAB), pl.Element(depth), pl.Element(d)),
                               lambda j: (start(j), 0, 0))],
        out_specs=out_spec,
        out_shape=out_shape,
        compiler_params=_params(("parallel",)),
        name="stage_w_in",
    )(w_t)


def _proj_kernel(x_ref, w_ref, wt_ref, rw_ref, cv_ref, gla_ref, fk_ref, fqt_ref, fvt_ref, fzt_ref):
    xb = x_ref[...].astype(BF16)
    p = jnp.dot(xb, w_ref[...], preferred_element_type=F32)
    off = 0
    for ref, n in ((rw_ref, RW_PAD), (cv_ref, CV_COLS), (gla_ref, GLA_PAD), (fk_ref, WIDTH)):
        ref[...] = p[:, off:off + n]
        off += n
    fqt_ref[...] = _bdot_nt(wt_ref[0:WIDTH, :], xb)
    fvt_ref[...] = _bdot_nt(wt_ref[WIDTH:2 * WIDTH, :], xb)
    fzt_ref[...] = _bdot_nt(wt_ref[2 * WIDTH:2 * WIDTH + FOX_ZROWS, :], xb)[0:8]


def _in_proj(x2, w_small, w_fox_t, l, tm):
    m = x2.shape[0]
    widths = (RW_PAD, CV_COLS, GLA_PAD, WIDTH)
    heights = (WIDTH, WIDTH, 8)
    return pl.pallas_call(
        _proj_kernel,
        grid=(m // tm,),
        in_specs=[pl.BlockSpec((tm, D_MODEL), lambda i: (i, 0)),
                  _layer_spec((D_MODEL, SMALL_COLS), l), _layer_spec((FOX_T_ROWS, D_MODEL), l)],
        out_specs=[pl.BlockSpec((tm, n), lambda i: (i, 0)) for n in widths]
        + [pl.BlockSpec((r, tm), lambda i: (0, i)) for r in heights],
        out_shape=[jax.ShapeDtypeStruct((m, n), F32) for n in widths]
        + [jax.ShapeDtypeStruct((r, m), F32) for r in heights],
        compiler_params=_params(("parallel",)),
        name="in_proj",
    )(x2, w_small, w_fox_t)


def _rwkv_kernel(p_ref, mu_ref, vec_ref, w2_ref, a2_ref, g2_ref, o_ref, st_ref, prev_ref, *, nb, tb):
    c = WIDTH
    n = CHUNK
    seqs = range(nb)

    @pl.when(pl.program_id(1) == 0)
    def _():
        st_ref[...] = jnp.zeros_like(st_ref)
        prev_ref[...] = jnp.zeros_like(prev_ref)

    row = lax.broadcasted_iota(jnp.int32, (tb, 1), 0)
    shifted = []
    for b in seqs:
        p = p_ref[b]
        prev = jnp.where(row == 0, prev_ref[b], pltpu.roll(p, 1, axis=0))
        prev_ref[b] = p[tb - 1:tb, :]
        shifted.append(p + (prev - p) * mu_ref[...])
    ps = jnp.concatenate(shifted, axis=0)

    w0, a0, k_k, k_a = vec_ref[0:1], vec_ref[1:2], vec_ref[2:3], vec_ref[3:4]
    r_k, ln_g, ln_b = vec_ref[4:5], vec_ref[5:6], vec_ref[6:7]

    r = ps[:, 0:c]
    k = ps[:, c:2 * c]
    v = ps[:, 2 * c:3 * c]
    wa = ps[:, 3 * c:3 * c + 128]
    xg = ps[:, 3 * c + 128:3 * c + 384]

    w_log = -_softplus(-(w0 + _bdot(jnp.tanh(wa), w2_ref[...]))) - 0.5
    logw = -jnp.exp(w_log)
    a_sig = _sigmoid(a0 + _bdot(wa, a2_ref[...]))
    g = _bdot(_sigmoid(xg), g2_ref[...])

    same, ri, ci = _head_masks(c, n, c, n)
    tr, tc = ri % n, ci % n
    same_f = jnp.where(same, 1.0, 0.0)
    ones_bd = same_f.astype(BF16)
    strict_bd = jnp.where(same & (tr > tc), 1.0, 0.0).astype(BF16)
    incl_bd = jnp.where(same & (tr >= tc), 1.0, 0.0).astype(BF16)
    eye = jnp.where(same & (tr == tc), 1.0, 0.0)
    tril = incl_bd[0:n, 0:n]
    sc_mask = jnp.concatenate([jnp.concatenate([strict_bd, strict_bd], axis=1),
                               jnp.concatenate([incl_bd, incl_bd], axis=1)], axis=0)

    kk = k * k_k
    kk = kk / jnp.maximum(jnp.sqrt(_dot3_right(kk * kk, ones_bd)), 1e-12)
    k_mod = k * (1.0 + (a_sig - 1.0) * k_a)
    a_vec = -kk
    b_vec = kk * a_sig

    mm = lambda x, y: jnp.dot(x, y, preferred_element_type=F32)
    rep = lambda x: _stack4(x.astype(BF16))
    bd = lambda x: rep(x) * ones_bd

    ys = [[] for _ in seqs]
    for ch in range(tb // n):
        sls = [slice(b * tb + ch * n, b * tb + (ch + 1) * n) for b in seqs]
        lw = [logw[s] for s in sls]
        cum = [_dot3_left(tril, x) for x in lw]
        last = [x[n - 1:n, :] for x in cum]
        g_inv = [jnp.exp(-x) for x in cum]
        g_end = [jnp.exp(l - x) for l, x in zip(last, cum)]
        at = [a_vec[s] * jnp.exp(x - w) for s, x, w in zip(sls, cum, lw)]
        rt = [r[s] * jnp.exp(x) for s, x in zip(sls, cum)]
        bt = [b_vec[s] * gi for s, gi in zip(sls, g_inv)]
        kt = [k_mod[s] * gi for s, gi in zip(sls, g_inv)]
        bh = [b_vec[s] * ge for s, ge in zip(sls, g_end)]
        kh = [k_mod[s] * ge for s, ge in zip(sls, g_end)]

        lhs = [jnp.concatenate([bd(x), bd(y)], axis=0) for x, y in zip(at, rt)]
        rhs = [jnp.concatenate([rep(x), rep(y)], axis=0) for x, y in zip(bt, kt)]
        sc = [_bdot_nt(x, y).astype(BF16) * sc_mask for x, y in zip(lhs, rhs)]
        m_ab = [x[0:c, 0:c] for x in sc]
        m_ak = [x[0:c, c:2 * c] for x in sc]
        m_rb = [x[c:2 * c, 0:c] for x in sc]
        m_rk = [x[c:2 * c, c:2 * c] for x in sc]

        t_inv = [eye + x.astype(F32) for x in m_ab]
        m_pow = m_ab
        for _ in range(5):
            m_pow = [mm(x, x).astype(BF16) for x in m_pow]
            t_inv = [t + mm(x, t.astype(BF16)) for x, t in zip(m_pow, t_inv)]
        t_inv = [t.astype(BF16) for t in t_inv]

        bdv = [bd(v[s]) for s in sls]
        pk = [mm(x, y) for x, y in zip(m_ak, bdv)]
        yk = [mm(x, y) for x, y in zip(m_rk, bdv)]
        st = [st_ref[b] for b in seqs]
        ws = [_bdot_nt(x, s) for x, s in zip(lhs, st)]
        u_bd = [mm(t, (w[0:c] + p_).astype(BF16)) for t, w, p_ in zip(t_inv, ws, pk)]
        y_bd = [w[c:2 * c] + mm(x, u.astype(BF16)) + z for w, x, u, z in zip(ws, m_rb, u_bd, yk)]
        upd = [_bdot_tn(jnp.concatenate([_fold4(u, n), v[s]], axis=0), jnp.concatenate([x, y], axis=0))
               for u, s, x, y in zip(u_bd, sls, bh, kh)]
        for b in seqs:
            ys[b].append(_fold4(y_bd[b], n))
            st_ref[b] = st[b] * jnp.exp(last[b]) + upd[b] * same_f

    y = jnp.concatenate([blk for b in seqs for blk in ys[b]], axis=0)
    inv_n = 1.0 / HEAD_DIM
    mu_y = _dot3_right(y, ones_bd) * inv_n
    dy = y - mu_y
    var_y = _dot3_right(dy * dy, ones_bd) * inv_n
    yn = dy * lax.rsqrt(var_y + RW_LN_EPS) * ln_g + ln_b
    bonus = _dot3_right(r * k_mod * r_k, ones_bd) * v
    out = (yn + bonus) * g
    for b in seqs:
        o_ref[b] = out[b * tb:(b + 1) * tb]


def _rwkv(p_rw, mu, vec, w2, a2, g2, l, nb, tb):
    bsz, t, _ = p_rw.shape
    return pl.pallas_call(
        functools.partial(_rwkv_kernel, nb=nb, tb=tb),
        grid=(bsz // nb, t // tb),
        in_specs=[pl.BlockSpec((nb, tb, RW_PAD), lambda b, i: (b, i, 0)),
                  _layer_spec((1, RW_PAD), l), _layer_spec((8, WIDTH), l),
                  _layer_spec((128, WIDTH), l), _layer_spec((128, WIDTH), l),
                  _layer_spec((WIDTH, WIDTH), l)],
        out_specs=pl.BlockSpec((nb, tb, WIDTH), lambda b, i: (b, i, 0)),
        out_shape=jax.ShapeDtypeStruct((bsz, t, WIDTH), F32),
        scratch_shapes=[pltpu.VMEM((nb, WIDTH, WIDTH), F32), pltpu.VMEM((nb, 1, RW_PAD), F32)],
        compiler_params=_params(("parallel", "arbitrary")),
        name="rwkv7",
    )(p_rw, mu, vec, w2, a2, g2)


def _conv_kernel(p_ref, w_ref, vec_ref, o_ref, buf_ref, sh_ref, *, tb):
    @pl.when(pl.program_id(1) == 0)
    def _():
        buf_ref[0:CONV_HALO, :] = jnp.zeros((CONV_HALO, WIDTH), F32)

    p = p_ref[0]
    buf_ref[CONV_HALO:CONV_HALO + tb, :] = p[:, 0:WIDTH] * _sigmoid(p[:, WIDTH:2 * WIDTH])
    acc = jnp.zeros((tb, WIDTH), F32) + vec_ref[0:1]
    first = CONV_HALO - (CONV_WIDTH - 1)
    for s in range(8):
        last_a = (CONV_WIDTH - 1 - s) // 8
        span = tb + 8 * last_a
        sh_ref[0:span, :] = buf_ref[first + s:first + s + span, :]
        for a in range(last_a + 1):
            kx = 8 * a + s
            acc = acc + sh_ref[8 * a:8 * a + tb, :] * w_ref[kx:kx + 1, :]
    buf_ref[0:CONV_HALO, :] = buf_ref[tb:tb + CONV_HALO, :]
    u = _layer_norm(acc, vec_ref[1:2], vec_ref[2:3])
    o_ref[0] = u * _sigmoid(u)


def _conv(p_cv, w, vec, l, tb):
    bsz, t, _ = p_cv.shape
    return pl.pallas_call(
        functools.partial(_conv_kernel, tb=tb),
        grid=(bsz, t // tb),
        in_specs=[pl.BlockSpec((1, tb, CV_COLS), lambda b, i: (b, i, 0)),
                  _layer_spec((CONV_HALO, WIDTH), l), _layer_spec((8, WIDTH), l)],
        out_specs=pl.BlockSpec((1, tb, WIDTH), lambda b, i: (b, i, 0)),
        out_shape=jax.ShapeDtypeStruct((bsz, t, WIDTH), F32),
        scratch_shapes=[pltpu.VMEM((tb + CONV_HALO, WIDTH), F32), pltpu.VMEM((tb + CONV_HALO, WIDTH), F32)],
        compiler_params=_params(("parallel", "arbitrary")),
        name="conv",
    )(p_cv, w, vec)


def _gla_kernel(p_ref, a2_ref, vec_ref, lng_ref, o_ref, st_ref, *, nb, tb):
    c = WIDTH
    n = CHUNK
    dk = HEADS * GLA_DK

    seqs = range(nb)

    @pl.when(pl.program_id(1) == 0)
    def _():
        st_ref[...] = jnp.zeros_like(st_ref)

    p = jnp.concatenate([p_ref[b] for b in seqs], axis=0)
    q = p[:, 0:dk] * (GLA_DK ** -0.5)
    k = p[:, dk:2 * dk]
    v = p[:, 2 * dk:2 * dk + c]
    gr = p[:, 2 * dk + c:2 * dk + 2 * c]
    z = p[:, 2 * dk + 2 * c:2 * dk + 2 * c + 128]
    log_a = _log_sigmoid(_bdot(z, a2_ref[...]) + vec_ref[0:1]) / GLA_TAU

    same_q, _, _ = _head_masks(c, n, dk, GLA_DK)
    same, ri, ci = _head_masks(c, n, c, n)
    same_qf = jnp.where(same_q, 1.0, 0.0)
    same_qb = same_qf.astype(BF16)
    ones_bd = jnp.where(same, 1.0, 0.0).astype(BF16)
    incl_bd = jnp.where(same & ((ri % n) >= (ci % n)), 1.0, 0.0).astype(BF16)
    tril = incl_bd[0:n, 0:n]

    mm = lambda x, y: jnp.dot(x, y, preferred_element_type=F32)
    rep = lambda x: _stack4(x.astype(BF16))

    os_ = [[] for _ in seqs]
    for ch in range(tb // n):
        sls = [slice(b * tb + ch * n, b * tb + (ch + 1) * n) for b in seqs]
        cum = [_dot3_left(tril, log_a[s]) for s in sls]
        last = [x[n - 1:n, :] for x in cum]
        q_bd = [rep(q[s] * jnp.exp(x)) * same_qb for s, x in zip(sls, cum)]
        k_inv = [rep(k[s] * jnp.exp(-x)) for s, x in zip(sls, cum)]
        k_end = [k[s] * jnp.exp(l - x) for s, l, x in zip(sls, last, cum)]
        sc = [jnp.where(incl_bd > 0, _bdot_nt(x, y).astype(BF16), 0.0) for x, y in zip(q_bd, k_inv)]
        st = [st_ref[b] for b in seqs]
        o_bd = [mm(x, rep(v[s]) * ones_bd) + _bdot_nt(y, z) for x, s, y, z in zip(sc, sls, q_bd, st)]
        upd = [_bdot_tn(v[s], x) for s, x in zip(sls, k_end)]
        for b in seqs:
            os_[b].append(_fold4(o_bd[b], n))
            st_ref[b] = st[b] * jnp.exp(last[b]) + upd[b] * same_qf

    o = jnp.concatenate([blk for b in seqs for blk in os_[b]], axis=0)
    ms = _dot3_right(o * o, ones_bd) * (1.0 / HEAD_DIM)
    o = o * lax.rsqrt(ms + GLA_EPS) * lng_ref[...]
    out = o * (gr * _sigmoid(gr))
    for b in seqs:
        o_ref[b] = out[b * tb:(b + 1) * tb]


def _gla(p_gla, a2, vec, ln_g, l, nb, tb):
    bsz, t, _ = p_gla.shape
    return pl.pallas_call(
        functools.partial(_gla_kernel, nb=nb, tb=tb),
        grid=(bsz // nb, t // tb),
        in_specs=[pl.BlockSpec((nb, tb, GLA_PAD), lambda b, i: (b, i, 0)),
                  _layer_spec((128, 128), l), _layer_spec((8, 128), l), _layer_spec((1, WIDTH), l)],
        out_specs=pl.BlockSpec((nb, tb, WIDTH), lambda b, i: (b, i, 0)),
        out_shape=jax.ShapeDtypeStruct((bsz, t, WIDTH), F32),
        scratch_shapes=[pltpu.VMEM((nb, WIDTH, HEADS * GLA_DK), F32)],
        compiler_params=_params(("parallel", "arbitrary")),
        name="gla",
    )(p_gla, a2, vec, ln_g)


def _fox_gate_kernel(z_ref, bf_ref, c_ref):
    x = _log_sigmoid(z_ref[...] + bf_ref[...])
    t = x.shape[1]
    lane = lax.broadcasted_iota(jnp.int32, x.shape, 1)
    sh = 1
    while sh < t:
        x = x + jnp.where(lane >= sh, pltpu.roll(x, sh, axis=1), 0.0)
        sh *= 2
    c_ref[...] = x


def _fox_gate(z_rows, bf, l, t):
    hp, m = z_rows.shape
    return pl.pallas_call(
        _fox_gate_kernel,
        grid=(m // t,),
        in_specs=[pl.BlockSpec((hp, t), lambda b: (0, b)), _layer_spec((hp, 1), l)],
        out_specs=pl.BlockSpec((hp, t), lambda b: (0, b)),
        out_shape=jax.ShapeDtypeStruct((hp, m), F32),
        compiler_params=_params(("parallel",)),
        name="fox_gate",
    )(z_rows, bf)


def _fox_kernel(qt_ref, k_ref, vt_ref, crow_ref, ccol_ref, o_ref, *, tq):
    c = WIDTH
    i = pl.program_id(1)
    qt = qt_ref[...] * (HEAD_DIM ** -0.5 * LOG2E)
    row_head = lax.broadcasted_iota(jnp.int32, (c, 1), 0) // HEAD_DIM
    qtm = jnp.concatenate([jnp.where(row_head == h, qt, 0.0) for h in range(HEADS)], axis=1).astype(BF16)
    cq = [crow_ref[h:h + 1, :] * LOG2E for h in range(HEADS)]
    causal = (lax.broadcasted_iota(jnp.int32, (tq, tq), 0)
              <= lax.broadcasted_iota(jnp.int32, (tq, tq), 1))

    def step(ks, tk, carry, masked):
        kblk = k_ref[pl.ds(ks, tk), :].astype(BF16)
        s_all = jnp.dot(kblk, qtm, preferred_element_type=F32)
        stats = []
        for h in range(HEADS):
            m, l, _ = carry[h]
            ck = ccol_ref[pl.ds(ks, tk), h:h + 1] * LOG2E
            s = s_all[:, h * tq:(h + 1) * tq] - ck
            if masked:
                s = jnp.where(causal, s, NEG_BIG)
            m_new = jnp.maximum(m, jnp.max(s, axis=0, keepdims=True) + cq[h])
            alpha = jnp.exp2(m - m_new)
            pr = jnp.exp2(s - (m_new - cq[h]))
            l = alpha * l + jnp.sum(pr, axis=0, keepdims=True)
            stats.append((m_new, l, alpha, pr.astype(BF16)))
        new = []
        for h in range(HEADS):
            m_new, l, alpha, pr = stats[h]
            vt_h = vt_ref[h * HEAD_DIM:(h + 1) * HEAD_DIM, pl.ds(ks, tk)].astype(BF16)
            acc = alpha * carry[h][2] + jnp.dot(vt_h, pr, preferred_element_type=F32)
            new.append((m_new, l, acc))
        return tuple(new)

    init = tuple((jnp.full((1, tq), NEG_BIG, F32), jnp.zeros((1, tq), F32),
                  jnp.zeros((HEAD_DIM, tq), F32)) for _ in range(HEADS))
    pairs = i // 2
    carry = lax.fori_loop(
        0, pairs, lambda kb, cr: step(pl.multiple_of(kb * 2 * tq, 2 * tq), 2 * tq, cr, False), init)
    carry = lax.fori_loop(
        2 * pairs, i, lambda kb, cr: step(pl.multiple_of(kb * tq, tq), tq, cr, False), carry)
    carry = step(pl.multiple_of(i * tq, tq), tq, carry, True)
    out_t = jnp.concatenate([acc / l for _, l, acc in carry], axis=0)
    o_ref[...] = out_t.T


def _fox(q_t, k, v_t, c_row, c_col, t, tq):
    m = k.shape[0]
    nq = t // tq
    hp = c_row.shape[0]
    return pl.pallas_call(
        functools.partial(_fox_kernel, tq=tq),
        grid=(m // t, nq),
        in_specs=[pl.BlockSpec((WIDTH, tq), lambda b, i: (0, b * nq + i)),
                  pl.BlockSpec((t, WIDTH), lambda b, i: (b, 0)),
                  pl.BlockSpec((WIDTH, t), lambda b, i: (0, b)),
                  pl.BlockSpec((hp, tq), lambda b, i: (0, b * nq + i)),
                  pl.BlockSpec((t, hp), lambda b, i: (b, 0))],
        out_specs=pl.BlockSpec((tq, WIDTH), lambda b, i: (b * nq + i, 0)),
        out_shape=jax.ShapeDtypeStruct((m, WIDTH), F32),
        compiler_params=_params(("parallel", "arbitrary")),
        name="fox",
    )(q_t, k, v_t, c_row, c_col)


def _merge_kernel(x_ref, y0_ref, y1_ref, y2_ref, y3_ref, wg_ref, gb_ref, u0_ref, u1_ref, u2_ref, u3_ref,
                  wo_ref, ln_ref, o_ref, *, alpha):
    x = x_ref[...]
    xb = x.astype(BF16)
    merged = None
    branches = ((y0_ref, u0_ref), (y1_ref, u1_ref), (y2_ref, u2_ref), (y3_ref, u3_ref))
    for j, (y_ref, up_ref) in enumerate(branches):
        gate = _sigmoid(jnp.dot(xb, wg_ref[:, j * D_MODEL:(j + 1) * D_MODEL],
                                preferred_element_type=F32) + gb_ref[j:j + 1, :])
        term = gate * _bdot(y_ref[...], up_ref[...])
        merged = term if merged is None else merged + term
    z = alpha * x + _bdot(merged, wo_ref[...])
    o_ref[...] = _layer_norm(z, ln_ref[0:1], ln_ref[1:2])


def _merge(x2, ys, wg, gb, ups, wo, ln, l, alpha, tm):
    m = x2.shape[0]
    row = lambda n: pl.BlockSpec((tm, n), lambda i: (i, 0))
    return pl.pallas_call(
        functools.partial(_merge_kernel, alpha=alpha),
        grid=(m // tm,),
        in_specs=[row(D_MODEL)] + [row(WIDTH)] * 4 + [
            _layer_spec((D_MODEL, GATE_COLS), l), _layer_spec((4, D_MODEL), l)]
        + [_layer_spec((WIDTH, D_MODEL), l)] * 4 + [
            _layer_spec((D_MODEL, D_MODEL), l), _layer_spec((2, D_MODEL), l, 0)],
        out_specs=row(D_MODEL),
        out_shape=jax.ShapeDtypeStruct((m, D_MODEL), F32),
        compiler_params=_params(("parallel",)),
        name="merge",
    )(x2, *ys, wg, gb, *ups, wo, ln)


def _xattn_kernel(x_ref, k_ref, v_ref, wq_ref, wo_ref, ln_ref, o_ref, *, alpha):
    x = x_ref[0]
    q = _bdot(x, wq_ref[...])
    hd = D_MODEL // HEADS
    sls = [slice(h * hd, (h + 1) * hd) for h in range(HEADS)]
    ss = [_bdot_nt(q[:, sl], k_ref[0, :, sl]) * (hd ** -0.5) for sl in sls]
    es = [jnp.exp(s - jnp.max(s, axis=-1, keepdims=True)) for s in ss]
    prs = [e / jnp.sum(e, axis=-1, keepdims=True) for e in es]
    outs = [_bdot(pr, v_ref[0, :, sl]) for pr, sl in zip(prs, sls)]
    y = _bdot(jnp.concatenate(outs, axis=1), wo_ref[...])
    o_ref[0] = _layer_norm(alpha * x + y, ln_ref[0:1], ln_ref[1:2])


def _xattn(x, k, v, wq, wo, ln, l, alpha, tm):
    bsz, t, _ = x.shape
    mlen = k.shape[2]
    return pl.pallas_call(
        functools.partial(_xattn_kernel, alpha=alpha),
        grid=(bsz, t // tm),
        in_specs=[pl.BlockSpec((1, tm, D_MODEL), lambda b, i: (b, i, 0)),
                  pl.BlockSpec((None, 1, mlen, D_MODEL), lambda b, i: (l, b, 0, 0)),
                  pl.BlockSpec((None, 1, mlen, D_MODEL), lambda b, i: (l, b, 0, 0)),
                  _layer_spec((D_MODEL, D_MODEL), l), _layer_spec((D_MODEL, D_MODEL), l),
                  _layer_spec((2, D_MODEL), l, 1)],
        out_specs=pl.BlockSpec((1, tm, D_MODEL), lambda b, i: (b, i, 0)),
        out_shape=jax.ShapeDtypeStruct((bsz, t, D_MODEL), F32),
        compiler_params=_params(("parallel", "parallel")),
        name="xattn",
    )(x, k, v, wq, wo, ln)


def _ffn_kernel(x_ref, w1_ref, w3_ref, w2_ref, ln_ref, o_ref, *, alpha):
    x = x_ref[...]
    xb = x.astype(BF16)
    y = None
    for c0, cw in FF_CHUNKS:
        h1 = jnp.dot(xb, w1_ref[:, c0:c0 + cw], preferred_element_type=F32)
        h3 = jnp.dot(xb, w3_ref[:, c0:c0 + cw], preferred_element_type=F32)
        part = _bdot(h1 * _sigmoid(h1) * h3, w2_ref[c0:c0 + cw, :])
        y = part if y is None else y + part
    o_ref[...] = _layer_norm(alpha * x + y, ln_ref[0:1], ln_ref[1:2])


def _ffn(x2, w1, w3, w2, ln, l, alpha, tm):
    m = x2.shape[0]
    return pl.pallas_call(
        functools.partial(_ffn_kernel, alpha=alpha),
        grid=(m // tm,),
        in_specs=[pl.BlockSpec((tm, D_MODEL), lambda i: (i, 0)),
                  _layer_spec((D_MODEL, D_FF), l), _layer_spec((D_MODEL, D_FF), l),
                  _layer_spec((D_FF, D_MODEL), l), _layer_spec((2, D_MODEL), l, 2)],
        out_specs=pl.BlockSpec((tm, D_MODEL), lambda i: (i, 0)),
        out_shape=jax.ShapeDtypeStruct((m, D_MODEL), F32),
        compiler_params=_params(("parallel",)),
        name="ffn",
    )(x2, w1, w3, w2, ln)


def _pad_axis(a, axis, n, before=0):
    widths = [(0, 0)] * a.ndim
    widths[axis] = (before, n - a.shape[axis] - before)
    return jnp.pad(a, widths)


def _rows8(*vs):
    depth = vs[0].shape[0]
    return _pad_axis(jnp.stack([v.reshape(depth, -1) for v in vs], axis=1), 1, 8)


def kernel(x, mem, w_in, rw_mu, rw_w0, rw_w2, rw_a0, rw_a2, rw_g2, rw_kk, rw_ka, rw_rk, rw_ln_g, rw_ln_b, rw_up, cv_w, cv_b, cv_ln_g, cv_ln_b, cv_up, gla_a2, gla_ab, gla_ln_g, gla_up, fox_bf, fox_up, gate_b, w_out, xa_wq, xa_wk, xa_wv, xa_wo, ffn_w1, ffn_w3, ffn_w2, ln_g, ln_b):
    bsz, t, d = x.shape
    depth = w_in.shape[0]
    alpha = (2.0 * depth) ** 0.25
    m = bsz * t
    mlen = mem.shape[1]
    tm = min(512, t)
    tb = min(256, t)
    nb = max(g for g in (8, 4, 2, 1) if bsz % g == 0)
    mem2 = mem.reshape(bsz * mlen, d)

    rw_vec = _rows8(rw_w0, rw_a0, rw_kk, rw_ka, rw_rk, rw_ln_g, rw_ln_b)
    rw_w2p = _pad_axis(rw_w2, 1, 128).astype(BF16)
    rw_a2p = _pad_axis(rw_a2, 1, 128, before=128 - rw_a2.shape[1]).astype(BF16)
    rw_g2p = _pad_axis(rw_g2, 1, WIDTH).astype(BF16)
    mu = _pad_axis(rw_mu, 1, RW_PAD)[:, None, :]
    cv_vec = _rows8(cv_b, cv_ln_g, cv_ln_b)
    cv_wp = _pad_axis(cv_w, 1, CONV_HALO)
    gla_a2p = _pad_axis(gla_a2, 1, 128).astype(BF16)
    gla_vec = _rows8(gla_ab)
    gla_lng = gla_ln_g[:, None, :]
    bf = _pad_axis(fox_bf, 1, 8)[:, :, None]
    lns = jnp.stack([ln_g, ln_b], axis=2)

    o_cv = RW_COLS
    o_gla = o_cv + CV_COLS
    o_fq = o_gla + GLA_COLS
    o_fk, o_fv, o_fz = o_fq + WIDTH, o_fq + 2 * WIDTH, o_fq + 3 * WIDTH
    o_gate = o_fq + FOX_COLS
    w_t = jnp.transpose(w_in, (2, 0, 1))
    w_small = _stage_w_in(w_t, ((0, RW_COLS), (o_cv, CV_COLS), (o_gla, GLA_COLS), (o_fk, WIDTH)), True)
    w_gate = _stage_w_in(w_t, ((o_gate, GATE_COLS),), True)
    w_fox_t = _stage_w_in(w_t, ((o_fq, WIDTH), (o_fv, WIDTH), (o_fz, HEADS)), False)
    ups = [_to_bf16(w) for w in (rw_up, cv_up, gla_up, fox_up)]
    wo_b, xq_b, xk_b, xv_b, xo_b = (_to_bf16(w) for w in (w_out, xa_wq, xa_wk, xa_wv, xa_wo))
    w1_b, w3_b, w2_b = (_to_bf16(w) for w in (ffn_w1, ffn_w3, ffn_w2))
    k_mem = _matmul_layers(mem2, xk_b, BF16).reshape(depth, bsz, mlen, d)
    v_mem = _matmul_layers(mem2, xv_b, BF16).reshape(depth, bsz, mlen, d)

    for l in range(depth):
        x2 = x.reshape(m, d)
        p_rw, p_cv, p_gla, f_k, f_qt, f_vt, f_zt = _in_proj(x2, w_small, w_fox_t, l, tm)
        p_rw = p_rw.reshape(bsz, t, RW_PAD)
        p_cv = p_cv.reshape(bsz, t, CV_COLS)
        p_gla = p_gla.reshape(bsz, t, GLA_PAD)

        y_rw = _rwkv(p_rw, mu, rw_vec, rw_w2p, rw_a2p, rw_g2p, l, nb, CHUNK)
        y_cv = _conv(p_cv, cv_wp, cv_vec, l, tb)
        y_gla = _gla(p_gla, gla_a2p, gla_vec, gla_lng, l, nb, CHUNK)
        c_row = _fox_gate(f_zt, bf, l, t)
        y_fox = _fox(f_qt, f_k, f_vt, c_row, c_row.T, t, tb)

        ys = [y.reshape(m, WIDTH) for y in (y_rw, y_cv, y_gla)] + [y_fox]
        x2 = _merge(x2, ys, w_gate, gate_b, ups, wo_b, lns, l, alpha, tm)

        x3 = _xattn(x2.reshape(bsz, t, d), k_mem, v_mem, xq_b, xo_b, lns, l, alpha, tm)

        x2 = _ffn(x3.reshape(m, d), w1_b, w3_b, w2_b, lns, l, alpha, tm)
        x = x2.reshape(bsz, t, d)
    return x
```

```python
import functools

import jax
import jax.numpy as jnp
from jax import lax
from jax.experimental import pallas as pl
from jax.experimental.pallas import tpu as pltpu

F32 = jnp.float32
BF16 = jnp.bfloat16

D_MODEL = 1024
WIDTH = 256
HEADS = 4
HEAD_DIM = WIDTH // HEADS
CHUNK = 64
RW_COLS = 1056
RW_PAD = 1152
CV_COLS = 512
GLA_COLS = 784
GLA_PAD = 896
GLA_DK = 32
FOX_COLS = 772
FOX_ZROWS = 16
FEAT_SLAB = 128
FOX_T_ROWS = 2 * WIDTH + FEAT_SLAB
SMALL_COLS = RW_PAD + CV_COLS + GLA_PAD + WIDTH
GATE_COLS = 4 * D_MODEL
CONV_WIDTH = 31
CONV_HALO = 32
D_FF = 2816
FF_CHUNKS = ((0, 1280), (1280, 1536))
RW_LN_EPS = 64e-5
GLA_EPS = 1e-5
GLA_TAU = 16.0
LN_EPS = 1e-5
NEG_BIG = -1e30
LOG2E = 1.4426950408889634
VMEM_LIMIT = 56 * 1024 * 1024


def _bdot(a, b):
    return jnp.dot(a.astype(BF16), b.astype(BF16), preferred_element_type=F32)


def _bdot_nt(a, b):
    return lax.dot_general(a.astype(BF16), b.astype(BF16), (((1,), (1,)), ((), ())),
                           preferred_element_type=F32)


def _bdot_tn(a, b):
    return lax.dot_general(a.astype(BF16), b.astype(BF16), (((0,), (0,)), ((), ())),
                           preferred_element_type=F32)


def _split3(x):
    hi = x.astype(BF16)
    r1 = x - hi.astype(F32)
    mid = r1.astype(BF16)
    lo = (r1 - mid.astype(F32)).astype(BF16)
    return hi, mid, lo


def _dot3_right(x, m):
    return sum(jnp.dot(p, m, preferred_element_type=F32) for p in _split3(x))


def _dot3_left(m, x):
    return sum(jnp.dot(m, p, preferred_element_type=F32) for p in _split3(x))


def _softplus(z):
    return jnp.maximum(z, 0.0) + jnp.log1p(jnp.exp(-jnp.abs(z)))


def _log_sigmoid(z):
    return -_softplus(-z)


def _sigmoid(z):
    return jax.nn.sigmoid(z)


def _layer_norm(z, g, b):
    mu = jnp.mean(z, axis=-1, keepdims=True)
    d = z - mu
    var = jnp.mean(d * d, axis=-1, keepdims=True)
    return d * lax.rsqrt(var + LN_EPS) * g + b


def _head_masks(rows, row_group, cols, col_group):
    ri = lax.broadcasted_iota(jnp.int32, (rows, cols), 0)
    ci = lax.broadcasted_iota(jnp.int32, (rows, cols), 1)
    same = (ri // row_group) == (ci // col_group)
    return same, ri, ci


def _stack4(x):
    return jnp.concatenate([x, x, x, x], axis=0)


def _fold4(x, n):
    return x[0:n] + x[n:2 * n] + x[2 * n:3 * n] + x[3 * n:4 * n]


def _layer_spec(shape, *lead):
    nd = len(shape)
    return pl.BlockSpec((None,) * len(lead) + tuple(shape), lambda *_: tuple(lead) + (0,) * nd,
                        pipeline_mode=pl.Buffered(1))


def _params(sem):
    return pltpu.CompilerParams(dimension_semantics=sem, vmem_limit_bytes=VMEM_LIMIT)


def _mm_kernel(a_ref, w_ref, o_ref):
    o_ref[...] = _bdot(a_ref[...], w_ref[...]).astype(o_ref.dtype)


def _matmul_layers(a, w, out_dtype):
    m, k = a.shape
    depth, _, n = w.shape
    return pl.pallas_call(
        _mm_kernel,
        grid=(depth,),
        in_specs=[pl.BlockSpec((m, k), lambda l: (0, 0)), pl.BlockSpec((None, k, n), lambda l: (l, 0, 0))],
        out_specs=pl.BlockSpec((None, m, n), lambda l: (l, 0, 0)),
        out_shape=jax.ShapeDtypeStruct((depth, m, n), out_dtype),
        compiler_params=_params(("parallel",)),
        name="matmul",
    )(a, w)


def _cast_kernel(w_ref, o_ref):
    o_ref[...] = w_ref[...].astype(o_ref.dtype)


def _to_bf16(w):
    depth, r, c = w.shape
    tr = r if r * c <= 2 * 1024 * 1024 else 256
    return pl.pallas_call(
        _cast_kernel,
        grid=(depth, r // tr),
        in_specs=[pl.BlockSpec((None, tr, c), lambda l, i: (l, i, 0))],
        out_specs=pl.BlockSpec((None, tr, c), lambda l, i: (l, i, 0)),
        out_shape=jax.ShapeDtypeStruct((depth, r, c), BF16),
        compiler_params=_params(("parallel", "parallel")),
        name="cast_bf16",
    )(w)


def _stage_kernel(w_ref, o_ref, *, pieces, transpose):
    j = pl.program_id(0)
    valid = jnp.int32(0)
    first = 0
    for _, n in pieces:
        chunks = -(-n // FEAT_SLAB)
        valid = jnp.where((j >= first) & (j < first + chunks), n - (j - first) * FEAT_SLAB, valid)
        first += chunks
    row = lax.broadcasted_iota(jnp.int32, (FEAT_SLAB, 1), 0)
    for l in range(w_ref.shape[1]):
        blk = jnp.where(row < valid, w_ref[:, l, :], 0.0)
        o_ref[l] = (blk.T if transpose else blk).astype(o_ref.dtype)


def _stage_w_in(w_t, pieces, transpose):
    n_feat, depth, d = w_t.shape
    total = sum(-(-n // FEAT_SLAB) for _, n in pieces)

    def start(j):
        s, first = jnp.int32(0), 0
        for p0, n in pieces:
            chunks = -(-n // FEAT_SLAB)
            s = jnp.where((j >= first) & (j < first + chunks), p0 + (j - first) * FEAT_SLAB, s)
            first += chunks
        return s

    if transpose:
        out_spec = pl.BlockSpec((depth, d, FEAT_SLAB), lambda j: (0, 0, j))
        out_shape = jax.ShapeDtypeStruct((depth, d, total * FEAT_SLAB), BF16)
    else:
        out_spec = pl.BlockSpec((depth, FEAT_SLAB, d), lambda j: (0, j, 0))
        out_shape = jax.ShapeDtypeStruct((depth, total * FEAT_SLAB, d), BF16)
    return pl.pallas_call(
        functools.partial(_stage_kernel, pieces=pieces, transpose=transpose),
        grid=(total,),
        in_specs=[pl.BlockSpec((pl.Element(FEAT_SLAB), pl.Element(depth), pl.Element(d)),
                               lambda j: (start(j), 0, 0))],
        out_specs=out_spec,
        out_shape=out_shape,
        compiler_params=_params(("parallel",)),
        name="stage_w_in",
    )(w_t)


def _proj_kernel(x_ref, w_ref, wt_ref, rw_ref, cv_ref, gla_ref, fk_ref, fqt_ref, fvt_ref, fzt_ref):
    xb = x_ref[...].astype(BF16)
    p = jnp.dot(xb, w_ref[...], preferred_element_type=F32)
    off = 0
    for ref, n in ((rw_ref, RW_PAD), (cv_ref, CV_COLS), (gla_ref, GLA_PAD), (fk_ref, WIDTH)):
        ref[...] = p[:, off:off + n]
        off += n
    fqt_ref[...] = _bdot_nt(wt_ref[0:WIDTH, :], xb)
    fvt_ref[...] = _bdot_nt(wt_ref[WIDTH:2 * WIDTH, :], xb)
    fzt_ref[...] = _bdot_nt(wt_ref[2 * WIDTH:2 * WIDTH + FOX_ZROWS, :], xb)[0:8]


def _in_proj(x2, w_small, w_fox_t, l, tm):
    m = x2.shape[0]
    widths = (RW_PAD, CV_COLS, GLA_PAD, WIDTH)
    heights = (WIDTH, WIDTH, 8)
    return pl.pallas_call(
        _proj_kernel,
        grid=(m // tm,),
        in_specs=[pl.BlockSpec((tm, D_MODEL), lambda i: (i, 0)),
                  _layer_spec((D_MODEL, SMALL_COLS), l), _layer_spec((FOX_T_ROWS, D_MODEL), l)],
        out_specs=[pl.BlockSpec((tm, n), lambda i: (i, 0)) for n in widths]
        + [pl.BlockSpec((r, tm), lambda i: (0, i)) for r in heights],
        out_shape=[jax.ShapeDtypeStruct((m, n), F32) for n in widths]
        + [jax.ShapeDtypeStruct((r, m), F32) for r in heights],
        compiler_params=_params(("parallel",)),
        name="in_proj",
    )(x2, w_small, w_fox_t)


def _rwkv_kernel(p_ref, mu_ref, vec_ref, w2_ref, a2_ref, g2_ref, o_ref, st_ref, prev_ref, *, nb, tb):
    c = WIDTH
    n = CHUNK
    seqs = range(nb)

    @pl.when(pl.program_id(1) == 0)
    def _():
        st_ref[...] = jnp.zeros_like(st_ref)
        prev_ref[...] = jnp.zeros_like(prev_ref)

    row = lax.broadcasted_iota(jnp.int32, (tb, 1), 0)
    shifted = []
    for b in seqs:
        p = p_ref[b]
        prev = jnp.where(row == 0, prev_ref[b], pltpu.roll(p, 1, axis=0))
        prev_ref[b] = p[tb - 1:tb, :]
        shifted.append(p + (prev - p) * mu_ref[...])
    ps = jnp.concatenate(shifted, axis=0)

    w0, a0, k_k, k_a = vec_ref[0:1], vec_ref[1:2], vec_ref[2:3], vec_ref[3:4]
    r_k, ln_g, ln_b = vec_ref[4:5], vec_ref[5:6], vec_ref[6:7]

    r = ps[:, 0:c]
    k = ps[:, c:2 * c]
    v = ps[:, 2 * c:3 * c]
    wa = ps[:, 3 * c:3 * c + 128]
    xg = ps[:, 3 * c + 128:3 * c + 384]

    w_log = -_softplus(-(w0 + _bdot(jnp.tanh(wa), w2_ref[...]))) - 0.5
    logw = -jnp.exp(w_log)
    a_sig = _sigmoid(a0 + _bdot(wa, a2_ref[...]))
    g = _bdot(_sigmoid(xg), g2_ref[...])

    same, ri, ci = _head_masks(c, n, c, n)
    tr, tc = ri % n, ci % n
    same_f = jnp.where(same, 1.0, 0.0)
    ones_bd = same_f.astype(BF16)
    strict_bd = jnp.where(same & (tr > tc), 1.0, 0.0).astype(BF16)
    incl_bd = jnp.where(same & (tr >= tc), 1.0, 0.0).astype(BF16)
    eye = jnp.where(same & (tr == tc), 1.0, 0.0)
    tril = incl_bd[0:n, 0:n]
    sc_mask = jnp.concatenate([jnp.concatenate([strict_bd, strict_bd], axis=1),
                               jnp.concatenate([incl_bd, incl_bd], axis=1)], axis=0)

    kk = k * k_k
    kk = kk / jnp.maximum(jnp.sqrt(_dot3_right(kk * kk, ones_bd)), 1e-12)
    k_mod = k * (1.0 + (a_sig - 1.0) * k_a)
    a_vec = -kk
    b_vec = kk * a_sig

    mm = lambda x, y: jnp.dot(x, y, preferred_element_type=F32)
    rep = lambda x: _stack4(x.astype(BF16))
    bd = lambda x: rep(x) * ones_bd

    ys = [[] for _ in seqs]
    for ch in range(tb // n):
        sls = [slice(b * tb + ch * n, b * tb + (ch + 1) * n) for b in seqs]
        lw = [logw[s] for s in sls]
        cum = [_dot3_left(tril, x) for x in lw]
        last = [x[n - 1:n, :] for x in cum]
        g_inv = [jnp.exp(-x) for x in cum]
        g_end = [jnp.exp(l - x) for l, x in zip(last, cum)]
        at = [a_vec[s] * jnp.exp(x - w) for s, x, w in zip(sls, cum, lw)]
        rt = [r[s] * jnp.exp(x) for s, x in zip(sls, cum)]
        bt = [b_vec[s] * gi for s, gi in zip(sls, g_inv)]
        kt = [k_mod[s] * gi for s, gi in zip(sls, g_inv)]
        bh = [b_vec[s] * ge for s, ge in zip(sls, g_end)]
        kh = [k_mod[s] * ge for s, ge in zip(sls, g_end)]

        lhs = [jnp.concatenate([bd(x), bd(y)], axis=0) for x, y in zip(at, rt)]
        rhs = [jnp.concatenate([rep(x), rep(y)], axis=0) for x, y in zip(bt, kt)]
        sc = [_bdot_nt(x, y).astype(BF16) * sc_mask for x, y in zip(lhs, rhs)]
        m_ab = [x[0:c, 0:c] for x in sc]
        m_ak = [x[0:c, c:2 * c] for x in sc]
        m_rb = [x[c:2 * c, 0:c] for x in sc]
        m_rk = [x[c:2 * c, c:2 * c] for x in sc]

        t_inv = [eye + x.astype(F32) for x in m_ab]
        m_pow = m_ab
        for _ in range(5):
            m_pow = [mm(x, x).astype(BF16) for x in m_pow]
            t_inv = [t + mm(x, t.astype(BF16)) for x, t in zip(m_pow, t_inv)]
        t_inv = [t.astype(BF16) for t in t_inv]

        bdv = [bd(v[s]) for s in sls]
        pk = [mm(x, y) for x, y in zip(m_ak, bdv)]
        yk = [mm(x, y) for x, y in zip(m_rk, bdv)]
        st = [st_ref[b] for b in seqs]
        ws = [_bdot_nt(x, s) for x, s in zip(lhs, st)]
        u_bd = [mm(t, (w[0:c] + p_).astype(BF16)) for t, w, p_ in zip(t_inv, ws, pk)]
        y_bd = [w[c:2 * c] + mm(x, u.astype(BF16)) + z for w, x, u, z in zip(ws, m_rb, u_bd, yk)]
        upd = [_bdot_tn(jnp.concatenate([_fold4(u, n), v[s]], axis=0), jnp.concatenate([x, y], axis=0))
               for u, s, x, y in zip(u_bd, sls, bh, kh)]
        for b in seqs:
            ys[b].append(_fold4(y_bd[b], n))
            st_ref[b] = st[b] * jnp.exp(last[b]) + upd[b] * same_f

    y = jnp.concatenate([blk for b in seqs for blk in ys[b]], axis=0)
    inv_n = 1.0 / HEAD_DIM
    mu_y = _dot3_right(y, ones_bd) * inv_n
    dy = y - mu_y
    var_y = _dot3_right(dy * dy, ones_bd) * inv_n
    yn = dy * lax.rsqrt(var_y + RW_LN_EPS) * ln_g + ln_b
    bonus = _dot3_right(r * k_mod * r_k, ones_bd) * v
    out = (yn + bonus) * g
    for b in seqs:
        o_ref[b] = out[b * tb:(b + 1) * tb]


def _rwkv(p_rw, mu, vec, w2, a2, g2, l, nb, tb):
    bsz, t, _ = p_rw.shape
    return pl.pallas_call(
        functools.partial(_rwkv_kernel, nb=nb, tb=tb),
        grid=(bsz // nb, t // tb),
        in_specs=[pl.BlockSpec((nb, tb, RW_PAD), lambda b, i: (b, i, 0)),
                  _layer_spec((1, RW_PAD), l), _layer_spec((8, WIDTH), l),
                  _layer_spec((128, WIDTH), l), _layer_spec((128, WIDTH), l),
                  _layer_spec((WIDTH, WIDTH), l)],
        out_specs=pl.BlockSpec((nb, tb, WIDTH), lambda b, i: (b, i, 0)),
        out_shape=jax.ShapeDtypeStruct((bsz, t, WIDTH), F32),
        scratch_shapes=[pltpu.VMEM((nb, WIDTH, WIDTH), F32), pltpu.VMEM((nb, 1, RW_PAD), F32)],
        compiler_params=_params(("parallel", "arbitrary")),
        name="rwkv7",
    )(p_rw, mu, vec, w2, a2, g2)


def _conv_kernel(p_ref, w_ref, vec_ref, o_ref, buf_ref, sh_ref, *, tb):
    @pl.when(pl.program_id(1) == 0)
    def _():
        buf_ref[0:CONV_HALO, :] = jnp.zeros((CONV_HALO, WIDTH), F32)

    p = p_ref[0]
    buf_ref[CONV_HALO:CONV_HALO + tb, :] = p[:, 0:WIDTH] * _sigmoid(p[:, WIDTH:2 * WIDTH])
    acc = jnp.zeros((tb, WIDTH), F32) + vec_ref[0:1]
    first = CONV_HALO - (CONV_WIDTH - 1)
    for s in range(8):
        last_a = (CONV_WIDTH - 1 - s) // 8
        span = tb + 8 * last_a
        sh_ref[0:span, :] = buf_ref[first + s:first + s + span, :]
        for a in range(last_a + 1):
            kx = 8 * a + s
            acc = acc + sh_ref[8 * a:8 * a + tb, :] * w_ref[kx:kx + 1, :]
    buf_ref[0:CONV_HALO, :] = buf_ref[tb:tb + CONV_HALO, :]
    u = _layer_norm(acc, vec_ref[1:2], vec_ref[2:3])
    o_ref[0] = u * _sigmoid(u)


def _conv(p_cv, w, vec, l, tb):
    bsz, t, _ = p_cv.shape
    return pl.pallas_call(
        functools.partial(_conv_kernel, tb=tb),
        grid=(bsz, t // tb),
        in_specs=[pl.BlockSpec((1, tb, CV_COLS), lambda b, i: (b, i, 0)),
                  _layer_spec((CONV_HALO, WIDTH), l), _layer_spec((8, WIDTH), l)],
        out_specs=pl.BlockSpec((1, tb, WIDTH), lambda b, i: (b, i, 0)),
        out_shape=jax.ShapeDtypeStruct((bsz, t, WIDTH), F32),
        scratch_shapes=[pltpu.VMEM((tb + CONV_HALO, WIDTH), F32), pltpu.VMEM((tb + CONV_HALO, WIDTH), F32)],
        compiler_params=_params(("parallel", "arbitrary")),
        name="conv",
    )(p_cv, w, vec)


def _gla_kernel(p_ref, a2_ref, vec_ref, lng_ref, o_ref, st_ref, *, nb, tb):
    c = WIDTH
    n = CHUNK
    dk = HEADS * GLA_DK

    seqs = range(nb)

    @pl.when(pl.program_id(1) == 0)
    def _():
        st_ref[...] = jnp.zeros_like(st_ref)

    p = jnp.concatenate([p_ref[b] for b in seqs], axis=0)
    q = p[:, 0:dk] * (GLA_DK ** -0.5)
    k = p[:, dk:2 * dk]
    v = p[:, 2 * dk:2 * dk + c]
    gr = p[:, 2 * dk + c:2 * dk + 2 * c]
    z = p[:, 2 * dk + 2 * c:2 * dk + 2 * c + 128]
    log_a = _log_sigmoid(_bdot(z, a2_ref[...]) + vec_ref[0:1]) / GLA_TAU

    same_q, _, _ = _head_masks(c, n, dk, GLA_DK)
    same, ri, ci = _head_masks(c, n, c, n)
    same_qf = jnp.where(same_q, 1.0, 0.0)
    same_qb = same_qf.astype(BF16)
    ones_bd = jnp.where(same, 1.0, 0.0).astype(BF16)
    incl_bd = jnp.where(same & ((ri % n) >= (ci % n)), 1.0, 0.0).astype(BF16)
    tril = incl_bd[0:n, 0:n]

    mm = lambda x, y: jnp.dot(x, y, preferred_element_type=F32)
    rep = lambda x: _stack4(x.astype(BF16))

    os_ = [[] for _ in seqs]
    for ch in range(tb // n):
        sls = [slice(b * tb + ch * n, b * tb + (ch + 1) * n) for b in seqs]
        cum = [_dot3_left(tril, log_a[s]) for s in sls]
        last = [x[n - 1:n, :] for x in cum]
        q_bd = [rep(q[s] * jnp.exp(x)) * same_qb for s, x in zip(sls, cum)]
        k_inv = [rep(k[s] * jnp.exp(-x)) for s, x in zip(sls, cum)]
        k_end = [k[s] * jnp.exp(l - x) for s, l, x in zip(sls, last, cum)]
        sc = [jnp.where(incl_bd > 0, _bdot_nt(x, y).astype(BF16), 0.0) for x, y in zip(q_bd, k_inv)]
        st = [st_ref[b] for b in seqs]
        o_bd = [mm(x, rep(v[s]) * ones_bd) + _bdot_nt(y, z) for x, s, y, z in zip(sc, sls, q_bd, st)]
        upd = [_bdot_tn(v[s], x) for s, x in zip(sls, k_end)]
        for b in seqs:
            os_[b].append(_fold4(o_bd[b], n))
            st_ref[b] = st[b] * jnp.exp(last[b]) + upd[b] * same_qf

    o = jnp.concatenate([blk for b in seqs for blk in os_[b]], axis=0)
    ms = _dot3_right(o * o, ones_bd) * (1.0 / HEAD_DIM)
    o = o * lax.rsqrt(ms + GLA_EPS) * lng_ref[...]
    out = o * (gr * _sigmoid(gr))
    for b in seqs:
        o_ref[b] = out[b * tb:(b + 1) * tb]


def _gla(p_gla, a2, vec, ln_g, l, nb, tb):
    bsz, t, _ = p_gla.shape
    return pl.pallas_call(
        functools.partial(_gla_kernel, nb=nb, tb=tb),
        grid=(bsz // nb, t // tb),
        in_specs=[pl.BlockSpec((nb, tb, GLA_PAD), lambda b, i: (b, i, 0)),
                  _layer_spec((128, 128), l), _layer_spec((8, 128), l), _layer_spec((1, WIDTH), l)],
        out_specs=pl.BlockSpec((nb, tb, WIDTH), lambda b, i: (b, i, 0)),
        out_shape=jax.ShapeDtypeStruct((bsz, t, WIDTH), F32),
        scratch_shapes=[pltpu.VMEM((nb, WIDTH, HEADS * GLA_DK), F32)],
        compiler_params=_params(("parallel", "arbitrary")),
        name="gla",
    )(p_gla, a2, vec, ln_g)


def _fox_gate_kernel(z_ref, bf_ref, c_ref):
    x = _log_sigmoid(z_ref[...] + bf_ref[...])
    t = x.shape[1]
    lane = lax.broadcasted_iota(jnp.int32, x.shape, 1)
    sh = 1
    while sh < t:
        x = x + jnp.where(lane >= sh, pltpu.roll(x, sh, axis=1), 0.0)
        sh *= 2
    c_ref[...] = x


def _fox_gate(z_rows, bf, l, t):
    hp, m = z_rows.shape
    return pl.pallas_call(
        _fox_gate_kernel,
        grid=(m // t,),
        in_specs=[pl.BlockSpec((hp, t), lambda b: (0, b)), _layer_spec((hp, 1), l)],
        out_specs=pl.BlockSpec((hp, t), lambda b: (0, b)),
        out_shape=jax.ShapeDtypeStruct((hp, m), F32),
        compiler_params=_params(("parallel",)),
        name="fox_gate",
    )(z_rows, bf)


def _fox_kernel(qt_ref, k_ref, vt_ref, crow_ref, ccol_ref, o_ref, s_ref, *, tq):
    c = WIDTH
    i = pl.program_id(1)
    qt = qt_ref[...] * (HEAD_DIM ** -0.5 * LOG2E)
    row_head = lax.broadcasted_iota(jnp.int32, (c, 1), 0) // HEAD_DIM
    qtm = jnp.concatenate([jnp.where(row_head == h, qt, 0.0) for h in range(HEADS)], axis=1).astype(BF16)
    cq = [crow_ref[h:h + 1, :] * LOG2E for h in range(HEADS)]
    causal = (lax.broadcasted_iota(jnp.int32, (tq, tq), 0)
              <= lax.broadcasted_iota(jnp.int32, (tq, tq), 1))

    def score(ks, slot):
        kblk = k_ref[pl.ds(ks, tq), :].astype(BF16)
        s_all = jnp.dot(kblk, qtm, preferred_element_type=F32)
        for h in range(HEADS):
            ck = ccol_ref[pl.ds(ks, tq), h:h + 1] * LOG2E
            s_ref[slot, :, h * tq:(h + 1) * tq] = s_all[:, h * tq:(h + 1) * tq] - ck

    def consume(ks, slot, carry, masked):
        tk = tq
        stats = []
        for h in range(HEADS):
            m, l, _ = carry[h]
            s = s_ref[slot, :, h * tq:(h + 1) * tq]
            if masked:
                s = jnp.where(causal, s, NEG_BIG)
            m_new = jnp.maximum(m, jnp.max(s, axis=0, keepdims=True) + cq[h])
            alpha = jnp.exp2(m - m_new)
            pr = jnp.exp2(s - (m_new - cq[h]))
            l = alpha * l + jnp.sum(pr, axis=0, keepdims=True)
            stats.append((m_new, l, alpha, pr.astype(BF16)))
        new = []
        for h in range(HEADS):
            m_new, l, alpha, pr = stats[h]
            vt_h = vt_ref[h * HEAD_DIM:(h + 1) * HEAD_DIM, pl.ds(ks, tk)].astype(BF16)
            acc = alpha * carry[h][2] + jnp.dot(vt_h, pr, preferred_element_type=F32)
            new.append((m_new, l, acc))
        return tuple(new)

    init = tuple((jnp.full((1, tq), NEG_BIG, F32), jnp.zeros((1, tq), F32),
                  jnp.zeros((HEAD_DIM, tq), F32)) for _ in range(HEADS))
    score(0, 0)

    def pair(j, cr):
        ks = pl.multiple_of(j * 2 * tq, 2 * tq)
        score(ks + tq, 1)
        cr = consume(ks, 0, cr, False)
        score(ks + 2 * tq, 0)
        return consume(ks + tq, 1, cr, False)

    pairs = i // 2
    carry = lax.fori_loop(0, pairs, pair, init)
    base = pl.multiple_of(pairs * 2 * tq, 2 * tq)

    def odd_tail(cr):
        score(base + tq, 1)
        return consume(base + tq, 1, consume(base, 0, cr, False), True)

    carry = lax.cond(i % 2 == 1, odd_tail, lambda cr: consume(base, 0, cr, True), carry)
    out_t = jnp.concatenate([acc / l for _, l, acc in carry], axis=0)
    o_ref[...] = out_t.T


def _fox(q_t, k, v_t, c_row, c_col, t, tq):
    m = k.shape[0]
    nq = t // tq
    hp = c_row.shape[0]
    return pl.pallas_call(
        functools.partial(_fox_kernel, tq=tq),
        grid=(m // t, nq),
        in_specs=[pl.BlockSpec((WIDTH, tq), lambda b, i: (0, b * nq + i)),
                  pl.BlockSpec((t, WIDTH), lambda b, i: (b, 0)),
                  pl.BlockSpec((WIDTH, t), lambda b, i: (0, b)),
                  pl.BlockSpec((hp, tq), lambda b, i: (0, b * nq + i)),
                  pl.BlockSpec((t, hp), lambda b, i: (b, 0))],
        out_specs=pl.BlockSpec((tq, WIDTH), lambda b, i: (b * nq + i, 0)),
        out_shape=jax.ShapeDtypeStruct((m, WIDTH), F32),
        scratch_shapes=[pltpu.VMEM((2, tq, HEADS * tq), F32)],
        compiler_params=_params(("parallel", "arbitrary")),
        name="fox",
    )(q_t, k, v_t, c_row, c_col)


def _merge_kernel(x_ref, y0_ref, y1_ref, y2_ref, y3_ref, wg_ref, gb_ref, u0_ref, u1_ref, u2_ref, u3_ref,
                  wo_ref, ln_ref, o_ref, *, alpha):
    x = x_ref[...]
    xb = x.astype(BF16)
    merged = None
    branches = ((y0_ref, u0_ref), (y1_ref, u1_ref), (y2_ref, u2_ref), (y3_ref, u3_ref))
    for j, (y_ref, up_ref) in enumerate(branches):
        gate = _sigmoid(jnp.dot(xb, wg_ref[:, j * D_MODEL:(j + 1) * D_MODEL],
                                preferred_element_type=F32) + gb_ref[j:j + 1, :])
        term = gate * _bdot(y_ref[...], up_ref[...])
        merged = term if merged is None else merged + term
    z = alpha * x + _bdot(merged, wo_ref[...])
    o_ref[...] = _layer_norm(z, ln_ref[0:1], ln_ref[1:2])


def _merge(x2, ys, wg, gb, ups, wo, ln, l, alpha, tm):
    m = x2.shape[0]
    row = lambda n: pl.BlockSpec((tm, n), lambda i: (i, 0))
    return pl.pallas_call(
        functools.partial(_merge_kernel, alpha=alpha),
        grid=(m // tm,),
        in_specs=[row(D_MODEL)] + [row(WIDTH)] * 4 + [
            _layer_spec((D_MODEL, GATE_COLS), l), _layer_spec((4, D_MODEL), l)]
        + [_layer_spec((WIDTH, D_MODEL), l)] * 4 + [
            _layer_spec((D_MODEL, D_MODEL), l), _layer_spec((2, D_MODEL), l, 0)],
        out_specs=row(D_MODEL),
        out_shape=jax.ShapeDtypeStruct((m, D_MODEL), F32),
        compiler_params=_params(("parallel",)),
        name="merge",
    )(x2, *ys, wg, gb, *ups, wo, ln)


def _xattn_kernel(x_ref, k_ref, v_ref, wq_ref, wo_ref, ln_ref, o_ref, *, alpha):
    x = x_ref[0]
    q = _bdot(x, wq_ref[...])
    hd = D_MODEL // HEADS
    sls = [slice(h * hd, (h + 1) * hd) for h in range(HEADS)]
    ss = [_bdot_nt(q[:, sl], k_ref[0, :, sl]) * (hd ** -0.5) for sl in sls]
    es = [jnp.exp(s - jnp.max(s, axis=-1, keepdims=True)) for s in ss]
    prs = [e / jnp.sum(e, axis=-1, keepdims=True) for e in es]
    outs = [_bdot(pr, v_ref[0, :, sl]) for pr, sl in zip(prs, sls)]
    y = _bdot(jnp.concatenate(outs, axis=1), wo_ref[...])
    o_ref[0] = _layer_norm(alpha * x + y, ln_ref[0:1], ln_ref[1:2])


def _xattn(x, k, v, wq, wo, ln, l, alpha, tm):
    bsz, t, _ = x.shape
    mlen = k.shape[2]
    return pl.pallas_call(
        functools.partial(_xattn_kernel, alpha=alpha),
        grid=(bsz, t // tm),
        in_specs=[pl.BlockSpec((1, tm, D_MODEL), lambda b, i: (b, i, 0)),
                  pl.BlockSpec((None, 1, mlen, D_MODEL), lambda b, i: (l, b, 0, 0)),
                  pl.BlockSpec((None, 1, mlen, D_MODEL), lambda b, i: (l, b, 0, 0)),
                  _layer_spec((D_MODEL, D_MODEL), l), _layer_spec((D_MODEL, D_MODEL), l),
                  _layer_spec((2, D_MODEL), l, 1)],
        out_specs=pl.BlockSpec((1, tm, D_MODEL), lambda b, i: (b, i, 0)),
        out_shape=jax.ShapeDtypeStruct((bsz, t, D_MODEL), F32),
        compiler_params=_params(("parallel", "parallel")),
        name="xattn",
    )(x, k, v, wq, wo, ln)


def _ffn_kernel(x_ref, w1_ref, w3_ref, w2_ref, ln_ref, o_ref, *, alpha):
    x = x_ref[...]
    xb = x.astype(BF16)
    y = None
    for c0, cw in FF_CHUNKS:
        h1 = jnp.dot(xb, w1_ref[:, c0:c0 + cw], preferred_element_type=F32)
        h3 = jnp.dot(xb, w3_ref[:, c0:c0 + cw], preferred_element_type=F32)
        part = _bdot(h1 * _sigmoid(h1) * h3, w2_ref[c0:c0 + cw, :])
        y = part if y is None else y + part
    o_ref[...] = _layer_norm(alpha * x + y, ln_ref[0:1], ln_ref[1:2])


def _ffn(x2, w1, w3, w2, ln, l, alpha, tm):
    m = x2.shape[0]
    return pl.pallas_call(
        functools.partial(_ffn_kernel, alpha=alpha),
        grid=(m // tm,),
        in_specs=[pl.BlockSpec((tm, D_MODEL), lambda i: (i, 0)),
                  _layer_spec((D_MODEL, D_FF), l), _layer_spec((D_MODEL, D_FF), l),
                  _layer_spec((D_FF, D_MODEL), l), _layer_spec((2, D_MODEL), l, 2)],
        out_specs=pl.BlockSpec((tm, D_MODEL), lambda i: (i, 0)),
        out_shape=jax.ShapeDtypeStruct((m, D_MODEL), F32),
        compiler_params=_params(("parallel",)),
        name="ffn",
    )(x2, w1, w3, w2, ln)


def _pad_axis(a, axis, n, before=0):
    widths = [(0, 0)] * a.ndim
    widths[axis] = (before, n - a.shape[axis] - before)
    return jnp.pad(a, widths)


def _rows8(*vs):
    depth = vs[0].shape[0]
    return _pad_axis(jnp.stack([v.reshape(depth, -1) for v in vs], axis=1), 1, 8)


def kernel(x, mem, w_in, rw_mu, rw_w0, rw_w2, rw_a0, rw_a2, rw_g2, rw_kk, rw_ka, rw_rk, rw_ln_g, rw_ln_b, rw_up, cv_w, cv_b, cv_ln_g, cv_ln_b, cv_up, gla_a2, gla_ab, gla_ln_g, gla_up, fox_bf, fox_up, gate_b, w_out, xa_wq, xa_wk, xa_wv, xa_wo, ffn_w1, ffn_w3, ffn_w2, ln_g, ln_b):
    bsz, t, d = x.shape
    depth = w_in.shape[0]
    alpha = (2.0 * depth) ** 0.25
    m = bsz * t
    mlen = mem.shape[1]
    tm = min(512, t)
    tb = min(256, t)
    nb = max(g for g in (8, 4, 2, 1) if bsz % g == 0)
    mem2 = mem.reshape(bsz * mlen, d)

    rw_vec = _rows8(rw_w0, rw_a0, rw_kk, rw_ka, rw_rk, rw_ln_g, rw_ln_b)
    rw_w2p = _pad_axis(rw_w2, 1, 128).astype(BF16)
    rw_a2p = _pad_axis(rw_a2, 1, 128, before=128 - rw_a2.shape[1]).astype(BF16)
    rw_g2p = _pad_axis(rw_g2, 1, WIDTH).astype(BF16)
    mu = _pad_axis(rw_mu, 1, RW_PAD)[:, None, :]
    cv_vec = _rows8(cv_b, cv_ln_g, cv_ln_b)
    cv_wp = _pad_axis(cv_w, 1, CONV_HALO)
    gla_a2p = _pad_axis(gla_a2, 1, 128).astype(BF16)
    gla_vec = _rows8(gla_ab)
    gla_lng = gla_ln_g[:, None, :]
    bf = _pad_axis(fox_bf, 1, 8)[:, :, None]
    lns = jnp.stack([ln_g, ln_b], axis=2)

    o_cv = RW_COLS
    o_gla = o_cv + CV_COLS
    o_fq = o_gla + GLA_COLS
    o_fk, o_fv, o_fz = o_fq + WIDTH, o_fq + 2 * WIDTH, o_fq + 3 * WIDTH
    o_gate = o_fq + FOX_COLS
    w_t = jnp.transpose(w_in, (2, 0, 1))
    w_small = _stage_w_in(w_t, ((0, RW_COLS), (o_cv, CV_COLS), (o_gla, GLA_COLS), (o_fk, WIDTH)), True)
    w_gate = _stage_w_in(w_t, ((o_gate, GATE_COLS),), True)
    w_fox_t = _stage_w_in(w_t, ((o_fq, WIDTH), (o_fv, WIDTH), (o_fz, HEADS)), False)
    ups = [_to_bf16(w) for w in (rw_up, cv_up, gla_up, fox_up)]
    wo_b, xq_b, xk_b, xv_b, xo_b = (_to_bf16(w) for w in (w_out, xa_wq, xa_wk, xa_wv, xa_wo))
    w1_b, w3_b, w2_b = (_to_bf16(w) for w in (ffn_w1, ffn_w3, ffn_w2))
    k_mem = _matmul_layers(mem2, xk_b, BF16).reshape(depth, bsz, mlen, d)
    v_mem = _matmul_layers(mem2, xv_b, BF16).reshape(depth, bsz, mlen, d)

    for l in range(depth):
        x2 = x.reshape(m, d)
        p_rw, p_cv, p_gla, f_k, f_qt, f_vt, f_zt = _in_proj(x2, w_small, w_fox_t, l, tm)
        p_rw = p_rw.reshape(bsz, t, RW_PAD)
        p_cv = p_cv.reshape(bsz, t, CV_COLS)
        p_gla = p_gla.reshape(bsz, t, GLA_PAD)

        y_rw = _rwkv(p_rw, mu, rw_vec, rw_w2p, rw_a2p, rw_g2p, l, nb, min(2 * CHUNK, t))
        y_cv = _conv(p_cv, cv_wp, cv_vec, l, tb)
        y_gla = _gla(p_gla, gla_a2p, gla_vec, gla_lng, l, nb, CHUNK)
        c_row = _fox_gate(f_zt, bf, l, t)
        y_fox = _fox(f_qt, f_k, f_vt, c_row, c_row.T, t, tb)

        ys = [y.reshape(m, WIDTH) for y in (y_rw, y_cv, y_gla)] + [y_fox]
        x2 = _merge(x2, ys, w_gate, gate_b, ups, wo_b, lns, l, alpha, tm)

        x3 = _xattn(x2.reshape(bsz, t, d), k_mem, v_mem, xq_b, xo_b, lns, l, alpha, tm)

        x2 = _ffn(x3.reshape(m, d), w1_b, w3_b, w2_b, lns, l, alpha, tm)
        x = x2.reshape(bsz, t, d)
    return x
```

```python
import functools

import jax
import jax.numpy as jnp
from jax import lax
from jax.experimental import pallas as pl
from jax.experimental.pallas import tpu as pltpu

F32 = jnp.float32
BF16 = jnp.bfloat16

D_MODEL = 1024
WIDTH = 256
HEADS = 4
HEAD_DIM = WIDTH // HEADS
CHUNK = 64
RW_COLS = 1056
RW_PAD = 1152
CV_COLS = 512
GLA_COLS = 784
GLA_PAD = 896
GLA_DK = 32
FOX_COLS = 772
FOX_ZROWS = 16
FEAT_SLAB = 128
FOX_T_ROWS = 2 * WIDTH + FEAT_SLAB
SMALL_COLS = RW_PAD + CV_COLS + GLA_PAD + WIDTH
GATE_COLS = 4 * D_MODEL
CONV_WIDTH = 31
CONV_HALO = 32
D_FF = 2816
FF_CHUNKS = ((0, 1280), (1280, 1536))
RW_LN_EPS = 64e-5
GLA_EPS = 1e-5
GLA_TAU = 16.0
LN_EPS = 1e-5
NEG_BIG = -1e30
LOG2E = 1.4426950408889634
DEN_ROWS = 16
VMEM_LIMIT = 56 * 1024 * 1024


def _bdot(a, b):
    return jnp.dot(a.astype(BF16), b.astype(BF16), preferred_element_type=F32)


def _bdot_nt(a, b):
    return lax.dot_general(a.astype(BF16), b.astype(BF16), (((1,), (1,)), ((), ())),
                           preferred_element_type=F32)


def _bdot_tn(a, b):
    return lax.dot_general(a.astype(BF16), b.astype(BF16), (((0,), (0,)), ((), ())),
                           preferred_element_type=F32)


def _split3(x):
    hi = x.astype(BF16)
    r1 = x - hi.astype(F32)
    mid = r1.astype(BF16)
    lo = (r1 - mid.astype(F32)).astype(BF16)
    return hi, mid, lo


def _dot3_right(x, m):
    return sum(jnp.dot(p, m, preferred_element_type=F32) for p in _split3(x))


def _dot3_left(m, x):
    return sum(jnp.dot(m, p, preferred_element_type=F32) for p in _split3(x))


def _softplus(z):
    return jnp.maximum(z, 0.0) + jnp.log1p(jnp.exp(-jnp.abs(z)))


def _log_sigmoid(z):
    return -_softplus(-z)


def _sigmoid(z):
    return jax.nn.sigmoid(z)


def _layer_norm(z, g, b):
    mu = jnp.mean(z, axis=-1, keepdims=True)
    d = z - mu
    var = jnp.mean(d * d, axis=-1, keepdims=True)
    return d * lax.rsqrt(var + LN_EPS) * g + b


def _head_masks(rows, row_group, cols, col_group):
    ri = lax.broadcasted_iota(jnp.int32, (rows, cols), 0)
    ci = lax.broadcasted_iota(jnp.int32, (rows, cols), 1)
    same = (ri // row_group) == (ci // col_group)
    return same, ri, ci


def _stack4(x):
    return jnp.concatenate([x, x, x, x], axis=0)


def _fold4(x, n):
    return x[0:n] + x[n:2 * n] + x[2 * n:3 * n] + x[3 * n:4 * n]


def _layer_spec(shape, *lead):
    nd = len(shape)
    return pl.BlockSpec((None,) * len(lead) + tuple(shape), lambda *_: tuple(lead) + (0,) * nd,
                        pipeline_mode=pl.Buffered(1))


def _params(sem):
    return pltpu.CompilerParams(dimension_semantics=sem, vmem_limit_bytes=VMEM_LIMIT)


def _mm_kernel(a_ref, w_ref, o_ref):
    o_ref[...] = _bdot(a_ref[...], w_ref[...]).astype(o_ref.dtype)


def _matmul_layers(a, w, out_dtype):
    m, k = a.shape
    depth, _, n = w.shape
    return pl.pallas_call(
        _mm_kernel,
        grid=(depth,),
        in_specs=[pl.BlockSpec((m, k), lambda l: (0, 0)), pl.BlockSpec((None, k, n), lambda l: (l, 0, 0))],
        out_specs=pl.BlockSpec((None, m, n), lambda l: (l, 0, 0)),
        out_shape=jax.ShapeDtypeStruct((depth, m, n), out_dtype),
        compiler_params=_params(("parallel",)),
        name="matmul",
    )(a, w)


def _cast_kernel(w_ref, o_ref):
    o_ref[...] = w_ref[...].astype(o_ref.dtype)


def _to_bf16(w):
    depth, r, c = w.shape
    tr = r if r * c <= 2 * 1024 * 1024 else 256
    return pl.pallas_call(
        _cast_kernel,
        grid=(depth, r // tr),
        in_specs=[pl.BlockSpec((None, tr, c), lambda l, i: (l, i, 0))],
        out_specs=pl.BlockSpec((None, tr, c), lambda l, i: (l, i, 0)),
        out_shape=jax.ShapeDtypeStruct((depth, r, c), BF16),
        compiler_params=_params(("parallel", "parallel")),
        name="cast_bf16",
    )(w)


def _stage_kernel(w_ref, o_ref, *, pieces, transpose):
    j = pl.program_id(0)
    valid = jnp.int32(0)
    first = 0
    for _, n in pieces:
        chunks = -(-n // FEAT_SLAB)
        valid = jnp.where((j >= first) & (j < first + chunks), n - (j - first) * FEAT_SLAB, valid)
        first += chunks
    row = lax.broadcasted_iota(jnp.int32, (FEAT_SLAB, 1), 0)
    for l in range(w_ref.shape[1]):
        blk = jnp.where(row < valid, w_ref[:, l, :], 0.0)
        o_ref[l] = (blk.T if transpose else blk).astype(o_ref.dtype)


def _stage_w_in(w_t, pieces, transpose):
    n_feat, depth, d = w_t.shape
    total = sum(-(-n // FEAT_SLAB) for _, n in pieces)

    def start(j):
        s, first = jnp.int32(0), 0
        for p0, n in pieces:
            chunks = -(-n // FEAT_SLAB)
            s = jnp.where((j >= first) & (j < first + chunks), p0 + (j - first) * FEAT_SLAB, s)
            first += chunks
        return s

    if transpose:
        out_spec = pl.BlockSpec((depth, d, FEAT_SLAB), lambda j: (0, 0, j))
        out_shape = jax.ShapeDtypeStruct((depth, d, total * FEAT_SLAB), BF16)
    else:
        out_spec = pl.BlockSpec((depth, FEAT_SLAB, d), lambda j: (0, j, 0))
        out_shape = jax.ShapeDtypeStruct((depth, total * FEAT_SLAB, d), BF16)
    return pl.pallas_call(
        functools.partial(_stage_kernel, pieces=pieces, transpose=transpose),
        grid=(total,),
        in_specs=[pl.BlockSpec((pl.Element(FEAT_SLAB), pl.Element(depth), pl.Element(d)),
                               lambda j: (start(j), 0, 0))],
        out_specs=out_spec,
        out_shape=out_shape,
        compiler_params=_params(("parallel",)),
        name="stage_w_in",
    )(w_t)


def _proj_kernel(x_ref, w_ref, wt_ref, rw_ref, cv_ref, gla_ref, fk_ref, fqt_ref, fvt_ref, fzt_ref):
    xb = x_ref[...].astype(BF16)
    p = jnp.dot(xb, w_ref[...], preferred_element_type=F32)
    off = 0
    for ref, n in ((rw_ref, RW_PAD), (cv_ref, CV_COLS), (gla_ref, GLA_PAD), (fk_ref, WIDTH)):
        ref[...] = p[:, off:off + n]
        off += n
    fqt_ref[...] = _bdot_nt(wt_ref[0:WIDTH, :], xb)
    fvt_ref[...] = _bdot_nt(wt_ref[WIDTH:2 * WIDTH, :], xb)
    fzt_ref[...] = _bdot_nt(wt_ref[2 * WIDTH:2 * WIDTH + FOX_ZROWS, :], xb)[0:8]


def _in_proj(x2, w_small, w_fox_t, l, tm):
    m = x2.shape[0]
    widths = (RW_PAD, CV_COLS, GLA_PAD, WIDTH)
    heights = (WIDTH, WIDTH, 8)
    return pl.pallas_call(
        _proj_kernel,
        grid=(m // tm,),
        in_specs=[pl.BlockSpec((tm, D_MODEL), lambda i: (i, 0)),
                  _layer_spec((D_MODEL, SMALL_COLS), l), _layer_spec((FOX_T_ROWS, D_MODEL), l)],
        out_specs=[pl.BlockSpec((tm, n), lambda i: (i, 0)) for n in widths]
        + [pl.BlockSpec((r, tm), lambda i: (0, i)) for r in heights],
        out_shape=[jax.ShapeDtypeStruct((m, n), F32) for n in widths]
        + [jax.ShapeDtypeStruct((r, m), F32) for r in heights],
        compiler_params=_params(("parallel",)),
        name="in_proj",
    )(x2, w_small, w_fox_t)


def _rwkv_kernel(p_ref, mu_ref, vec_ref, w2_ref, a2_ref, g2_ref, o_ref, st_ref, prev_ref, *, nb, tb):
    c = WIDTH
    n = CHUNK
    seqs = range(nb)

    @pl.when(pl.program_id(1) == 0)
    def _():
        st_ref[...] = jnp.zeros_like(st_ref)
        prev_ref[...] = jnp.zeros_like(prev_ref)

    row = lax.broadcasted_iota(jnp.int32, (tb, 1), 0)
    shifted = []
    for b in seqs:
        p = p_ref[b]
        prev = jnp.where(row == 0, prev_ref[b], pltpu.roll(p, 1, axis=0))
        prev_ref[b] = p[tb - 1:tb, :]
        shifted.append(p + (prev - p) * mu_ref[...])
    ps = jnp.concatenate(shifted, axis=0)

    w0, a0, k_k, k_a = vec_ref[0:1], vec_ref[1:2], vec_ref[2:3], vec_ref[3:4]
    r_k, ln_g, ln_b = vec_ref[4:5], vec_ref[5:6], vec_ref[6:7]

    r = ps[:, 0:c]
    k = ps[:, c:2 * c]
    v = ps[:, 2 * c:3 * c]
    wa = ps[:, 3 * c:3 * c + 128]
    xg = ps[:, 3 * c + 128:3 * c + 384]

    w_log = -_softplus(-(w0 + _bdot(jnp.tanh(wa), w2_ref[...]))) - 0.5
    logw = -jnp.exp(w_log)
    a_sig = _sigmoid(a0 + _bdot(wa, a2_ref[...]))
    g = _bdot(_sigmoid(xg), g2_ref[...])

    same, ri, ci = _head_masks(c, n, c, n)
    tr, tc = ri % n, ci % n
    same_f = jnp.where(same, 1.0, 0.0)
    ones_bd = same_f.astype(BF16)
    strict_bd = jnp.where(same & (tr > tc), 1.0, 0.0).astype(BF16)
    incl_bd = jnp.where(same & (tr >= tc), 1.0, 0.0).astype(BF16)
    eye = jnp.where(same & (tr == tc), 1.0, 0.0)
    tril = incl_bd[0:n, 0:n]
    sc_mask = jnp.concatenate([jnp.concatenate([strict_bd, strict_bd], axis=1),
                               jnp.concatenate([incl_bd, incl_bd], axis=1)], axis=0)

    kk = k * k_k
    kk = kk / jnp.maximum(jnp.sqrt(_dot3_right(kk * kk, ones_bd)), 1e-12)
    k_mod = k * (1.0 + (a_sig - 1.0) * k_a)
    a_vec = -kk
    b_vec = kk * a_sig

    mm = lambda x, y: jnp.dot(x, y, preferred_element_type=F32)
    rep = lambda x: _stack4(x.astype(BF16))
    bd = lambda x: rep(x) * ones_bd

    ys = [[] for _ in seqs]
    for ch in range(tb // n):
        sls = [slice(b * tb + ch * n, b * tb + (ch + 1) * n) for b in seqs]
        lw = [logw[s] for s in sls]
        cum = [_dot3_left(tril, x) for x in lw]
        last = [x[n - 1:n, :] for x in cum]
        g_inv = [jnp.exp(-x) for x in cum]
        g_end = [jnp.exp(l - x) for l, x in zip(last, cum)]
        at = [a_vec[s] * jnp.exp(x - w) for s, x, w in zip(sls, cum, lw)]
        rt = [r[s] * jnp.exp(x) for s, x in zip(sls, cum)]
        bt = [b_vec[s] * gi for s, gi in zip(sls, g_inv)]
        kt = [k_mod[s] * gi for s, gi in zip(sls, g_inv)]
        bh = [b_vec[s] * ge for s, ge in zip(sls, g_end)]
        kh = [k_mod[s] * ge for s, ge in zip(sls, g_end)]

        lhs = [jnp.concatenate([bd(x), bd(y)], axis=0) for x, y in zip(at, rt)]
        rhs = [jnp.concatenate([rep(x), rep(y)], axis=0) for x, y in zip(bt, kt)]
        sc = [_bdot_nt(x, y).astype(BF16) * sc_mask for x, y in zip(lhs, rhs)]
        m_ab = [x[0:c, 0:c] for x in sc]
        m_ak = [x[0:c, c:2 * c] for x in sc]
        m_rb = [x[c:2 * c, 0:c] for x in sc]
        m_rk = [x[c:2 * c, c:2 * c] for x in sc]

        t_inv = [eye + x.astype(F32) for x in m_ab]
        m_pow = m_ab
        for _ in range(5):
            m_pow = [mm(x, x).astype(BF16) for x in m_pow]
            t_inv = [t + mm(x, t.astype(BF16)) for x, t in zip(m_pow, t_inv)]
        t_inv = [t.astype(BF16) for t in t_inv]

        bdv = [bd(v[s]) for s in sls]
        pk = [mm(x, y) for x, y in zip(m_ak, bdv)]
        yk = [mm(x, y) for x, y in zip(m_rk, bdv)]
        st = [st_ref[b] for b in seqs]
        ws = [_bdot_nt(x, s) for x, s in zip(lhs, st)]
        u_bd = [mm(t, (w[0:c] + p_).astype(BF16)) for t, w, p_ in zip(t_inv, ws, pk)]
        y_bd = [w[c:2 * c] + mm(x, u.astype(BF16)) + z for w, x, u, z in zip(ws, m_rb, u_bd, yk)]
        upd = [_bdot_tn(jnp.concatenate([_fold4(u, n), v[s]], axis=0), jnp.concatenate([x, y], axis=0))
               for u, s, x, y in zip(u_bd, sls, bh, kh)]
        for b in seqs:
            ys[b].append(_fold4(y_bd[b], n))
            st_ref[b] = st[b] * jnp.exp(last[b]) + upd[b] * same_f

    y = jnp.concatenate([blk for b in seqs for blk in ys[b]], axis=0)
    inv_n = 1.0 / HEAD_DIM
    mu_y = _dot3_right(y, ones_bd) * inv_n
    dy = y - mu_y
    var_y = _dot3_right(dy * dy, ones_bd) * inv_n
    yn = dy * lax.rsqrt(var_y + RW_LN_EPS) * ln_g + ln_b
    bonus = _dot3_right(r * k_mod * r_k, ones_bd) * v
    out = (yn + bonus) * g
    for b in seqs:
        o_ref[b] = out[b * tb:(b + 1) * tb]


def _rwkv(p_rw, mu, vec, w2, a2, g2, l, nb, tb):
    bsz, t, _ = p_rw.shape
    return pl.pallas_call(
        functools.partial(_rwkv_kernel, nb=nb, tb=tb),
        grid=(bsz // nb, t // tb),
        in_specs=[pl.BlockSpec((nb, tb, RW_PAD), lambda b, i: (b, i, 0)),
                  _layer_spec((1, RW_PAD), l), _layer_spec((8, WIDTH), l),
                  _layer_spec((128, WIDTH), l), _layer_spec((128, WIDTH), l),
                  _layer_spec((WIDTH, WIDTH), l)],
        out_specs=pl.BlockSpec((nb, tb, WIDTH), lambda b, i: (b, i, 0)),
        out_shape=jax.ShapeDtypeStruct((bsz, t, WIDTH), F32),
        scratch_shapes=[pltpu.VMEM((nb, WIDTH, WIDTH), F32), pltpu.VMEM((nb, 1, RW_PAD), F32)],
        compiler_params=_params(("parallel", "arbitrary")),
        name="rwkv7",
    )(p_rw, mu, vec, w2, a2, g2)


def _conv_kernel(p_ref, w_ref, vec_ref, o_ref, buf_ref, sh_ref, *, tb):
    @pl.when(pl.program_id(1) == 0)
    def _():
        buf_ref[0:CONV_HALO, :] = jnp.zeros((CONV_HALO, WIDTH), F32)

    p = p_ref[0]
    buf_ref[CONV_HALO:CONV_HALO + tb, :] = p[:, 0:WIDTH] * _sigmoid(p[:, WIDTH:2 * WIDTH])
    acc = jnp.zeros((tb, WIDTH), F32) + vec_ref[0:1]
    first = CONV_HALO - (CONV_WIDTH - 1)
    for s in range(8):
        last_a = (CONV_WIDTH - 1 - s) // 8
        span = tb + 8 * last_a
        sh_ref[0:span, :] = buf_ref[first + s:first + s + span, :]
        for a in range(last_a + 1):
            kx = 8 * a + s
            acc = acc + sh_ref[8 * a:8 * a + tb, :] * w_ref[kx:kx + 1, :]
    buf_ref[0:CONV_HALO, :] = buf_ref[tb:tb + CONV_HALO, :]
    u = _layer_norm(acc, vec_ref[1:2], vec_ref[2:3])
    o_ref[0] = u * _sigmoid(u)


def _conv(p_cv, w, vec, l, tb):
    bsz, t, _ = p_cv.shape
    return pl.pallas_call(
        functools.partial(_conv_kernel, tb=tb),
        grid=(bsz, t // tb),
        in_specs=[pl.BlockSpec((1, tb, CV_COLS), lambda b, i: (b, i, 0)),
                  _layer_spec((CONV_HALO, WIDTH), l), _layer_spec((8, WIDTH), l)],
        out_specs=pl.BlockSpec((1, tb, WIDTH), lambda b, i: (b, i, 0)),
        out_shape=jax.ShapeDtypeStruct((bsz, t, WIDTH), F32),
        scratch_shapes=[pltpu.VMEM((tb + CONV_HALO, WIDTH), F32), pltpu.VMEM((tb + CONV_HALO, WIDTH), F32)],
        compiler_params=_params(("parallel", "arbitrary")),
        name="conv",
    )(p_cv, w, vec)


def _gla_kernel(p_ref, a2_ref, vec_ref, lng_ref, o_ref, st_ref, *, nb, tb):
    c = WIDTH
    n = CHUNK
    dk = HEADS * GLA_DK

    seqs = range(nb)

    @pl.when(pl.program_id(1) == 0)
    def _():
        st_ref[...] = jnp.zeros_like(st_ref)

    p = jnp.concatenate([p_ref[b] for b in seqs], axis=0)
    q = p[:, 0:dk] * (GLA_DK ** -0.5)
    k = p[:, dk:2 * dk]
    v = p[:, 2 * dk:2 * dk + c]
    gr = p[:, 2 * dk + c:2 * dk + 2 * c]
    z = p[:, 2 * dk + 2 * c:2 * dk + 2 * c + 128]
    log_a = _log_sigmoid(_bdot(z, a2_ref[...]) + vec_ref[0:1]) / GLA_TAU

    same_q, _, _ = _head_masks(c, n, dk, GLA_DK)
    same, ri, ci = _head_masks(c, n, c, n)
    same_qf = jnp.where(same_q, 1.0, 0.0)
    same_qb = same_qf.astype(BF16)
    ones_bd = jnp.where(same, 1.0, 0.0).astype(BF16)
    incl_bd = jnp.where(same & ((ri % n) >= (ci % n)), 1.0, 0.0).astype(BF16)
    tril = incl_bd[0:n, 0:n]

    mm = lambda x, y: jnp.dot(x, y, preferred_element_type=F32)
    rep = lambda x: _stack4(x.astype(BF16))

    os_ = [[] for _ in seqs]
    for ch in range(tb // n):
        sls = [slice(b * tb + ch * n, b * tb + (ch + 1) * n) for b in seqs]
        cum = [_dot3_left(tril, log_a[s]) for s in sls]
        last = [x[n - 1:n, :] for x in cum]
        q_bd = [rep(q[s] * jnp.exp(x)) * same_qb for s, x in zip(sls, cum)]
        k_inv = [rep(k[s] * jnp.exp(-x)) for s, x in zip(sls, cum)]
        k_end = [k[s] * jnp.exp(l - x) for s, l, x in zip(sls, last, cum)]
        sc = [jnp.where(incl_bd > 0, _bdot_nt(x, y).astype(BF16), 0.0) for x, y in zip(q_bd, k_inv)]
        st = [st_ref[b] for b in seqs]
        o_bd = [mm(x, rep(v[s]) * ones_bd) + _bdot_nt(y, z) for x, s, y, z in zip(sc, sls, q_bd, st)]
        upd = [_bdot_tn(v[s], x) for s, x in zip(sls, k_end)]
        for b in seqs:
            os_[b].append(_fold4(o_bd[b], n))
            st_ref[b] = st[b] * jnp.exp(last[b]) + upd[b] * same_qf

    o = jnp.concatenate([blk for b in seqs for blk in os_[b]], axis=0)
    ms = _dot3_right(o * o, ones_bd) * (1.0 / HEAD_DIM)
    o = o * lax.rsqrt(ms + GLA_EPS) * lng_ref[...]
    out = o * (gr * _sigmoid(gr))
    for b in seqs:
        o_ref[b] = out[b * tb:(b + 1) * tb]


def _gla(p_gla, a2, vec, ln_g, l, nb, tb):
    bsz, t, _ = p_gla.shape
    return pl.pallas_call(
        functools.partial(_gla_kernel, nb=nb, tb=tb),
        grid=(bsz // nb, t // tb),
        in_specs=[pl.BlockSpec((nb, tb, GLA_PAD), lambda b, i: (b, i, 0)),
                  _layer_spec((128, 128), l), _layer_spec((8, 128), l), _layer_spec((1, WIDTH), l)],
        out_specs=pl.BlockSpec((nb, tb, WIDTH), lambda b, i: (b, i, 0)),
        out_shape=jax.ShapeDtypeStruct((bsz, t, WIDTH), F32),
        scratch_shapes=[pltpu.VMEM((nb, WIDTH, HEADS * GLA_DK), F32)],
        compiler_params=_params(("parallel", "arbitrary")),
        name="gla",
    )(p_gla, a2, vec, ln_g)


def _fox_gate_kernel(z_ref, bf_ref, c_ref):
    x = _log_sigmoid(z_ref[...] + bf_ref[...])
    t = x.shape[1]
    lane = lax.broadcasted_iota(jnp.int32, x.shape, 1)
    sh = 1
    while sh < t:
        x = x + jnp.where(lane >= sh, pltpu.roll(x, sh, axis=1), 0.0)
        sh *= 2
    c_ref[...] = x


def _fox_gate(z_rows, bf, l, t):
    hp, m = z_rows.shape
    return pl.pallas_call(
        _fox_gate_kernel,
        grid=(m // t,),
        in_specs=[pl.BlockSpec((hp, t), lambda b: (0, b)), _layer_spec((hp, 1), l)],
        out_specs=pl.BlockSpec((hp, t), lambda b: (0, b)),
        out_shape=jax.ShapeDtypeStruct((hp, m), F32),
        compiler_params=_params(("parallel",)),
        name="fox_gate",
    )(z_rows, bf)


def _fox_kernel(qt_ref, k_ref, vt_ref, crow_ref, ccol_ref, o_ref, s_ref, *, tq):
    c = WIDTH
    i = pl.program_id(1)
    qt = qt_ref[...] * (HEAD_DIM ** -0.5 * LOG2E)
    row_head = lax.broadcasted_iota(jnp.int32, (c, 1), 0) // HEAD_DIM
    qtm = jnp.concatenate([jnp.where(row_head == h, qt, 0.0) for h in range(HEADS)], axis=1).astype(BF16)
    cq = [crow_ref[h:h + 1, :] * LOG2E for h in range(HEADS)]
    causal = (lax.broadcasted_iota(jnp.int32, (tq, tq), 0)
              <= lax.broadcasted_iota(jnp.int32, (tq, tq), 1))

    def score(ks, slot):
        kblk = k_ref[pl.ds(ks, tq), :].astype(BF16)
        s_all = jnp.dot(kblk, qtm, preferred_element_type=F32)
        for h in range(HEADS):
            ck = ccol_ref[pl.ds(ks, tq), h:h + 1] * LOG2E
            s_ref[slot, :, h * tq:(h + 1) * tq] = s_all[:, h * tq:(h + 1) * tq] - ck

    def consume(ks, slot, carry, masked):
        tk = tq
        stats = []
        for h in range(HEADS):
            m = carry[h][0]
            s = s_ref[slot, :, h * tq:(h + 1) * tq]
            if masked:
                s = jnp.where(causal, s, NEG_BIG)
            m_new = jnp.maximum(m, jnp.max(s, axis=0, keepdims=True) + cq[h])
            alpha = jnp.exp2(m - m_new)
            pr = jnp.exp2(s - (m_new - cq[h]))
            stats.append((m_new, alpha, pr.astype(BF16)))
        new = []
        for h in range(HEADS):
            m_new, alpha, pr = stats[h]
            vt_h = vt_ref[h * HEAD_DIM:(h + 1) * HEAD_DIM, pl.ds(ks, tk)].astype(BF16)
            vt_aug = jnp.concatenate([vt_h, jnp.ones((DEN_ROWS, tk), BF16)], axis=0)
            acc = alpha * carry[h][1] + jnp.dot(vt_aug, pr, preferred_element_type=F32)
            new.append((m_new, acc))
        return tuple(new)

    init = tuple((jnp.full((1, tq), NEG_BIG, F32), jnp.zeros((HEAD_DIM + DEN_ROWS, tq), F32))
                 for _ in range(HEADS))
    score(0, 0)

    def pair(j, cr):
        ks = pl.multiple_of(j * 2 * tq, 2 * tq)
        score(ks + tq, 1)
        cr = consume(ks, 0, cr, False)
        score(ks + 2 * tq, 0)
        return consume(ks + tq, 1, cr, False)

    pairs = i // 2
    carry = lax.fori_loop(0, pairs, pair, init)
    base = pl.multiple_of(pairs * 2 * tq, 2 * tq)

    def odd_tail(cr):
        score(base + tq, 1)
        return consume(base + tq, 1, consume(base, 0, cr, False), True)

    carry = lax.cond(i % 2 == 1, odd_tail, lambda cr: consume(base, 0, cr, True), carry)
    out_t = jnp.concatenate([acc[0:HEAD_DIM] / acc[HEAD_DIM:HEAD_DIM + 1] for _, acc in carry], axis=0)
    o_ref[...] = out_t.T


def _fox(q_t, k, v_t, c_row, c_col, t, tq):
    m = k.shape[0]
    nq = t // tq
    hp = c_row.shape[0]
    return pl.pallas_call(
        functools.partial(_fox_kernel, tq=tq),
        grid=(m // t, nq),
        in_specs=[pl.BlockSpec((WIDTH, tq), lambda b, i: (0, b * nq + i)),
                  pl.BlockSpec((t, WIDTH), lambda b, i: (b, 0)),
                  pl.BlockSpec((WIDTH, t), lambda b, i: (0, b)),
                  pl.BlockSpec((hp, tq), lambda b, i: (0, b * nq + i)),
                  pl.BlockSpec((t, hp), lambda b, i: (b, 0))],
        out_specs=pl.BlockSpec((tq, WIDTH), lambda b, i: (b * nq + i, 0)),
        out_shape=jax.ShapeDtypeStruct((m, WIDTH), F32),
        scratch_shapes=[pltpu.VMEM((2, tq, HEADS * tq), F32)],
        compiler_params=_params(("parallel", "arbitrary")),
        name="fox",
    )(q_t, k, v_t, c_row, c_col)


def _merge_kernel(x_ref, y0_ref, y1_ref, y2_ref, y3_ref, wg_ref, gb_ref, u0_ref, u1_ref, u2_ref, u3_ref,
                  wo_ref, ln_ref, o_ref, *, alpha):
    x = x_ref[...]
    xb = x.astype(BF16)
    merged = None
    branches = ((y0_ref, u0_ref), (y1_ref, u1_ref), (y2_ref, u2_ref), (y3_ref, u3_ref))
    for j, (y_ref, up_ref) in enumerate(branches):
        gate = _sigmoid(jnp.dot(xb, wg_ref[:, j * D_MODEL:(j + 1) * D_MODEL],
                                preferred_element_type=F32) + gb_ref[j:j + 1, :])
        term = gate * _bdot(y_ref[...], up_ref[...])
        merged = term if merged is None else merged + term
    z = alpha * x + _bdot(merged, wo_ref[...])
    o_ref[...] = _layer_norm(z, ln_ref[0:1], ln_ref[1:2])


def _merge(x2, ys, wg, gb, ups, wo, ln, l, alpha, tm):
    m = x2.shape[0]
    row = lambda n: pl.BlockSpec((tm, n), lambda i: (i, 0))
    return pl.pallas_call(
        functools.partial(_merge_kernel, alpha=alpha),
        grid=(m // tm,),
        in_specs=[row(D_MODEL)] + [row(WIDTH)] * 4 + [
            _layer_spec((D_MODEL, GATE_COLS), l), _layer_spec((4, D_MODEL), l)]
        + [_layer_spec((WIDTH, D_MODEL), l)] * 4 + [
            _layer_spec((D_MODEL, D_MODEL), l), _layer_spec((2, D_MODEL), l, 0)],
        out_specs=row(D_MODEL),
        out_shape=jax.ShapeDtypeStruct((m, D_MODEL), F32),
        compiler_params=_params(("parallel",)),
        name="merge",
    )(x2, *ys, wg, gb, *ups, wo, ln)


def _xattn_kernel(x_ref, k_ref, v_ref, wq_ref, wo_ref, ln_ref, o_ref, *, alpha):
    x = x_ref[0]
    q = _bdot(x, wq_ref[...])
    hd = D_MODEL // HEADS
    sls = [slice(h * hd, (h + 1) * hd) for h in range(HEADS)]
    ss = [_bdot_nt(q[:, sl], k_ref[0, :, sl]) * (hd ** -0.5) for sl in sls]
    es = [jnp.exp(s - jnp.max(s, axis=-1, keepdims=True)) for s in ss]
    prs = [e / jnp.sum(e, axis=-1, keepdims=True) for e in es]
    outs = [_bdot(pr, v_ref[0, :, sl]) for pr, sl in zip(prs, sls)]
    y = _bdot(jnp.concatenate(outs, axis=1), wo_ref[...])
    o_ref[0] = _layer_norm(alpha * x + y, ln_ref[0:1], ln_ref[1:2])


def _xattn(x, k, v, wq, wo, ln, l, alpha, tm):
    bsz, t, _ = x.shape
    mlen = k.shape[2]
    return pl.pallas_call(
        functools.partial(_xattn_kernel, alpha=alpha),
        grid=(bsz, t // tm),
        in_specs=[pl.BlockSpec((1, tm, D_MODEL), lambda b, i: (b, i, 0)),
                  pl.BlockSpec((None, 1, mlen, D_MODEL), lambda b, i: (l, b, 0, 0)),
                  pl.BlockSpec((None, 1, mlen, D_MODEL), lambda b, i: (l, b, 0, 0)),
                  _layer_spec((D_MODEL, D_MODEL), l), _layer_spec((D_MODEL, D_MODEL), l),
                  _layer_spec((2, D_MODEL), l, 1)],
        out_specs=pl.BlockSpec((1, tm, D_MODEL), lambda b, i: (b, i, 0)),
        out_shape=jax.ShapeDtypeStruct((bsz, t, D_MODEL), F32),
        compiler_params=_params(("parallel", "parallel")),
        name="xattn",
    )(x, k, v, wq, wo, ln)


def _ffn_kernel(x_ref, w1_ref, w3_ref, w2_ref, ln_ref, o_ref, *, alpha):
    x = x_ref[...]
    xb = x.astype(BF16)
    y = None
    for c0, cw in FF_CHUNKS:
        h1 = jnp.dot(xb, w1_ref[:, c0:c0 + cw], preferred_element_type=F32)
        h3 = jnp.dot(xb, w3_ref[:, c0:c0 + cw], preferred_element_type=F32)
        part = _bdot(h1 * _sigmoid(h1) * h3, w2_ref[c0:c0 + cw, :])
        y = part if y is None else y + part
    o_ref[...] = _layer_norm(alpha * x + y, ln_ref[0:1], ln_ref[1:2])


def _ffn(x2, w1, w3, w2, ln, l, alpha, tm):
    m = x2.shape[0]
    return pl.pallas_call(
        functools.partial(_ffn_kernel, alpha=alpha),
        grid=(m // tm,),
        in_specs=[pl.BlockSpec((tm, D_MODEL), lambda i: (i, 0)),
                  _layer_spec((D_MODEL, D_FF), l), _layer_spec((D_MODEL, D_FF), l),
                  _layer_spec((D_FF, D_MODEL), l), _layer_spec((2, D_MODEL), l, 2)],
        out_specs=pl.BlockSpec((tm, D_MODEL), lambda i: (i, 0)),
        out_shape=jax.ShapeDtypeStruct((m, D_MODEL), F32),
        compiler_params=_params(("parallel",)),
        name="ffn",
    )(x2, w1, w3, w2, ln)


def _pad_axis(a, axis, n, before=0):
    widths = [(0, 0)] * a.ndim
    widths[axis] = (before, n - a.shape[axis] - before)
    return jnp.pad(a, widths)


def _rows8(*vs):
    depth = vs[0].shape[0]
    return _pad_axis(jnp.stack([v.reshape(depth, -1) for v in vs], axis=1), 1, 8)


def kernel(x, mem, w_in, rw_mu, rw_w0, rw_w2, rw_a0, rw_a2, rw_g2, rw_kk, rw_ka, rw_rk, rw_ln_g, rw_ln_b, rw_up, cv_w, cv_b, cv_ln_g, cv_ln_b, cv_up, gla_a2, gla_ab, gla_ln_g, gla_up, fox_bf, fox_up, gate_b, w_out, xa_wq, xa_wk, xa_wv, xa_wo, ffn_w1, ffn_w3, ffn_w2, ln_g, ln_b):
    bsz, t, d = x.shape
    depth = w_in.shape[0]
    alpha = (2.0 * depth) ** 0.25
    m = bsz * t
    mlen = mem.shape[1]
    tm = min(512, t)
    tb = min(256, t)
    nb = max(g for g in (8, 4, 2, 1) if bsz % g == 0)
    mem2 = mem.reshape(bsz * mlen, d)

    rw_vec = _rows8(rw_w0, rw_a0, rw_kk, rw_ka, rw_rk, rw_ln_g, rw_ln_b)
    rw_w2p = _pad_axis(rw_w2, 1, 128).astype(BF16)
    rw_a2p = _pad_axis(rw_a2, 1, 128, before=128 - rw_a2.shape[1]).astype(BF16)
    rw_g2p = _pad_axis(rw_g2, 1, WIDTH).astype(BF16)
    mu = _pad_axis(rw_mu, 1, RW_PAD)[:, None, :]
    cv_vec = _rows8(cv_b, cv_ln_g, cv_ln_b)
    cv_wp = _pad_axis(cv_w, 1, CONV_HALO)
    gla_a2p = _pad_axis(gla_a2, 1, 128).astype(BF16)
    gla_vec = _rows8(gla_ab)
    gla_lng = gla_ln_g[:, None, :]
    bf = _pad_axis(fox_bf, 1, 8)[:, :, None]
    lns = jnp.stack([ln_g, ln_b], axis=2)

    o_cv = RW_COLS
    o_gla = o_cv + CV_COLS
    o_fq = o_gla + GLA_COLS
    o_fk, o_fv, o_fz = o_fq + WIDTH, o_fq + 2 * WIDTH, o_fq + 3 * WIDTH
    o_gate = o_fq + FOX_COLS
    w_t = jnp.transpose(w_in, (2, 0, 1))
    w_small = _stage_w_in(w_t, ((0, RW_COLS), (o_cv, CV_COLS), (o_gla, GLA_COLS), (o_fk, WIDTH)), True)
    w_gate = _stage_w_in(w_t, ((o_gate, GATE_COLS),), True)
    w_fox_t = _stage_w_in(w_t, ((o_fq, WIDTH), (o_fv, WIDTH), (o_fz, HEADS)), False)
    ups = [_to_bf16(w) for w in (rw_up, cv_up, gla_up, fox_up)]
    wo_b, xq_b, xk_b, xv_b, xo_b = (_to_bf16(w) for w in (w_out, xa_wq, xa_wk, xa_wv, xa_wo))
    w1_b, w3_b, w2_b = (_to_bf16(w) for w in (ffn_w1, ffn_w3, ffn_w2))
    k_mem = _matmul_layers(mem2, xk_b, BF16).reshape(depth, bsz, mlen, d)
    v_mem = _matmul_layers(mem2, xv_b, BF16).reshape(depth, bsz, mlen, d)

    for l in range(depth):
        x2 = x.reshape(m, d)
        p_rw, p_cv, p_gla, f_k, f_qt, f_vt, f_zt = _in_proj(x2, w_small, w_fox_t, l, tm)
        p_rw = p_rw.reshape(bsz, t, RW_PAD)
        p_cv = p_cv.reshape(bsz, t, CV_COLS)
        p_gla = p_gla.reshape(bsz, t, GLA_PAD)

        y_rw = _rwkv(p_rw, mu, rw_vec, rw_w2p, rw_a2p, rw_g2p, l, nb, min(2 * CHUNK, t))
        y_cv = _conv(p_cv, cv_wp, cv_vec, l, tb)
        y_gla = _gla(p_gla, gla_a2p, gla_vec, gla_lng, l, nb, CHUNK)
        c_row = _fox_gate(f_zt, bf, l, t)
        y_fox = _fox(f_qt, f_k, f_vt, c_row, c_row.T, t, tb)

        ys = [y.reshape(m, WIDTH) for y in (y_rw, y_cv, y_gla)] + [y_fox]
        x2 = _merge(x2, ys, w_gate, gate_b, ups, wo_b, lns, l, alpha, tm)

        x3 = _xattn(x2.reshape(bsz, t, d), k_mem, v_mem, xq_b, xo_b, lns, l, alpha, tm)

        x2 = _ffn(x3.reshape(m, d), w1_b, w3_b, w2_b, lns, l, alpha, tm)
        x = x2.reshape(bsz, t, d)
    return x
```

```python
import functools

import jax
import jax.numpy as jnp
from jax import lax
from jax.experimental import pallas as pl
from jax.experimental.pallas import tpu as pltpu

F32 = jnp.float32
BF16 = jnp.bfloat16

D_MODEL = 1024
WIDTH = 256
HEADS = 4
HEAD_DIM = WIDTH // HEADS
CHUNK = 64
RW_COLS = 1056
RW_PAD = 1152
CV_COLS = 512
GLA_COLS = 784
GLA_PAD = 896
GLA_DK = 32
FOX_COLS = 772
FOX_ZROWS = 16
FEAT_SLAB = 128
FOX_T_ROWS = 2 * WIDTH + FEAT_SLAB
SMALL_COLS = RW_PAD + CV_COLS + GLA_PAD + WIDTH
GATE_COLS = 4 * D_MODEL
CONV_WIDTH = 31
CONV_HALO = 32
D_FF = 2816
FF_CHUNKS = ((0, 1280), (1280, 1536))
RW_LN_EPS = 64e-5
GLA_EPS = 1e-5
GLA_TAU = 16.0
LN_EPS = 1e-5
NEG_BIG = -1e30
LOG2E = 1.4426950408889634
DEN_ROWS = 16
VMEM_LIMIT = 56 * 1024 * 1024


def _bdot(a, b):
    return jnp.dot(a.astype(BF16), b.astype(BF16), preferred_element_type=F32)


def _bdot_nt(a, b):
    return lax.dot_general(a.astype(BF16), b.astype(BF16), (((1,), (1,)), ((), ())),
                           preferred_element_type=F32)


def _bdot_tn(a, b):
    return lax.dot_general(a.astype(BF16), b.astype(BF16), (((0,), (0,)), ((), ())),
                           preferred_element_type=F32)


def _split3(x):
    hi = x.astype(BF16)
    r1 = x - hi.astype(F32)
    mid = r1.astype(BF16)
    lo = (r1 - mid.astype(F32)).astype(BF16)
    return hi, mid, lo


def _dot2_right(x, m):
    hi, mid, _ = _split3(x)
    return jnp.dot(hi, m, preferred_element_type=F32) + jnp.dot(mid, m, preferred_element_type=F32)


def _dot3_left(m, x):
    return sum(jnp.dot(m, p, preferred_element_type=F32) for p in _split3(x))


def _softplus(z):
    return jnp.maximum(z, 0.0) + jnp.log1p(jnp.exp(-jnp.abs(z)))


def _log_sigmoid(z):
    return -_softplus(-z)


def _sigmoid(z):
    return jax.nn.sigmoid(z)


def _layer_norm(z, g, b):
    mu = jnp.mean(z, axis=-1, keepdims=True)
    d = z - mu
    var = jnp.mean(d * d, axis=-1, keepdims=True)
    return d * lax.rsqrt(var + LN_EPS) * g + b


def _head_masks(rows, row_group, cols, col_group):
    ri = lax.broadcasted_iota(jnp.int32, (rows, cols), 0)
    ci = lax.broadcasted_iota(jnp.int32, (rows, cols), 1)
    same = (ri // row_group) == (ci // col_group)
    return same, ri, ci


def _stack4(x):
    return jnp.concatenate([x, x, x, x], axis=0)


def _fold4(x, n):
    return x[0:n] + x[n:2 * n] + x[2 * n:3 * n] + x[3 * n:4 * n]


def _layer_spec(shape, *lead):
    nd = len(shape)
    return pl.BlockSpec((None,) * len(lead) + tuple(shape), lambda *_: tuple(lead) + (0,) * nd,
                        pipeline_mode=pl.Buffered(1))


def _params(sem):
    return pltpu.CompilerParams(dimension_semantics=sem, vmem_limit_bytes=VMEM_LIMIT)


def _mm_kernel(a_ref, w_ref, o_ref):
    o_ref[...] = _bdot(a_ref[...], w_ref[...]).astype(o_ref.dtype)


def _matmul_layers(a, w, out_dtype):
    m, k = a.shape
    depth, _, n = w.shape
    return pl.pallas_call(
        _mm_kernel,
        grid=(depth,),
        in_specs=[pl.BlockSpec((m, k), lambda l: (0, 0)), pl.BlockSpec((None, k, n), lambda l: (l, 0, 0))],
        out_specs=pl.BlockSpec((None, m, n), lambda l: (l, 0, 0)),
        out_shape=jax.ShapeDtypeStruct((depth, m, n), out_dtype),
        compiler_params=_params(("parallel",)),
        name="matmul",
    )(a, w)


def _cast_kernel(w_ref, o_ref):
    o_ref[...] = w_ref[...].astype(o_ref.dtype)


def _to_bf16(w):
    depth, r, c = w.shape
    tr = r if r * c <= 2 * 1024 * 1024 else 256
    return pl.pallas_call(
        _cast_kernel,
        grid=(depth, r // tr),
        in_specs=[pl.BlockSpec((None, tr, c), lambda l, i: (l, i, 0))],
        out_specs=pl.BlockSpec((None, tr, c), lambda l, i: (l, i, 0)),
        out_shape=jax.ShapeDtypeStruct((depth, r, c), BF16),
        compiler_params=_params(("parallel", "parallel")),
        name="cast_bf16",
    )(w)


def _stage_kernel(w_ref, o_ref, *, pieces, transpose):
    j = pl.program_id(0)
    valid = jnp.int32(0)
    first = 0
    for _, n in pieces:
        chunks = -(-n // FEAT_SLAB)
        valid = jnp.where((j >= first) & (j < first + chunks), n - (j - first) * FEAT_SLAB, valid)
        first += chunks
    row = lax.broadcasted_iota(jnp.int32, (FEAT_SLAB, 1), 0)
    for l in range(w_ref.shape[1]):
        blk = jnp.where(row < valid, w_ref[:, l, :], 0.0)
        o_ref[l] = (blk.T if transpose else blk).astype(o_ref.dtype)


def _stage_w_in(w_t, pieces, transpose):
    n_feat, depth, d = w_t.shape
    total = sum(-(-n // FEAT_SLAB) for _, n in pieces)

    def start(j):
        s, first = jnp.int32(0), 0
        for p0, n in pieces:
            chunks = -(-n // FEAT_SLAB)
            s = jnp.where((j >= first) & (j < first + chunks), p0 + (j - first) * FEAT_SLAB, s)
            first += chunks
        return s

    if transpose:
        out_spec = pl.BlockSpec((depth, d, FEAT_SLAB), lambda j: (0, 0, j))
        out_shape = jax.ShapeDtypeStruct((depth, d, total * FEAT_SLAB), BF16)
    else:
        out_spec = pl.BlockSpec((depth, FEAT_SLAB, d), lambda j: (0, j, 0))
        out_shape = jax.ShapeDtypeStruct((depth, total * FEAT_SLAB, d), BF16)
    return pl.pallas_call(
        functools.partial(_stage_kernel, pieces=pieces, transpose=transpose),
        grid=(total,),
        in_specs=[pl.BlockSpec((pl.Element(FEAT_SLAB), pl.Element(depth), pl.Element(d)),
                               lambda j: (start(j), 0, 0))],
        out_specs=out_spec,
        out_shape=out_shape,
        compiler_params=_params(("parallel",)),
        name="stage_w_in",
    )(w_t)


def _proj_kernel(x_ref, w_ref, wt_ref, rw_ref, cv_ref, gla_ref, fk_ref, fqt_ref, fvt_ref, fzt_ref):
    xb = x_ref[...].astype(BF16)
    p = jnp.dot(xb, w_ref[...], preferred_element_type=F32)
    off = 0
    for ref, n in ((rw_ref, RW_PAD), (cv_ref, CV_COLS), (gla_ref, GLA_PAD), (fk_ref, WIDTH)):
        ref[...] = p[:, off:off + n]
        off += n
    fqt_ref[...] = _bdot_nt(wt_ref[0:WIDTH, :], xb)
    fvt_ref[...] = _bdot_nt(wt_ref[WIDTH:2 * WIDTH, :], xb)
    fzt_ref[...] = _bdot_nt(wt_ref[2 * WIDTH:2 * WIDTH + FOX_ZROWS, :], xb)[0:8]


def _in_proj(x2, w_small, w_fox_t, l, tm):
    m = x2.shape[0]
    widths = (RW_PAD, CV_COLS, GLA_PAD, WIDTH)
    heights = (WIDTH, WIDTH, 8)
    return pl.pallas_call(
        _proj_kernel,
        grid=(m // tm,),
        in_specs=[pl.BlockSpec((tm, D_MODEL), lambda i: (i, 0)),
                  _layer_spec((D_MODEL, SMALL_COLS), l), _layer_spec((FOX_T_ROWS, D_MODEL), l)],
        out_specs=[pl.BlockSpec((tm, n), lambda i: (i, 0)) for n in widths]
        + [pl.BlockSpec((r, tm), lambda i: (0, i)) for r in heights],
        out_shape=[jax.ShapeDtypeStruct((m, n), F32) for n in widths]
        + [jax.ShapeDtypeStruct((r, m), F32) for r in heights],
        compiler_params=_params(("parallel",)),
        name="in_proj",
    )(x2, w_small, w_fox_t)


def _rwkv_kernel(p_ref, mu_ref, vec_ref, w2_ref, a2_ref, g2_ref, o_ref, st_ref, prev_ref, *, nb, tb):
    c = WIDTH
    n = CHUNK
    seqs = range(nb)

    @pl.when(pl.program_id(1) == 0)
    def _():
        st_ref[...] = jnp.zeros_like(st_ref)
        prev_ref[...] = jnp.zeros_like(prev_ref)

    row = lax.broadcasted_iota(jnp.int32, (tb, 1), 0)
    shifted = []
    for b in seqs:
        p = p_ref[b]
        prev = jnp.where(row == 0, prev_ref[b], pltpu.roll(p, 1, axis=0))
        prev_ref[b] = p[tb - 1:tb, :]
        shifted.append(p + (prev - p) * mu_ref[...])
    ps = jnp.concatenate(shifted, axis=0)

    w0, a0, k_k, k_a = vec_ref[0:1], vec_ref[1:2], vec_ref[2:3], vec_ref[3:4]
    r_k, ln_g, ln_b = vec_ref[4:5], vec_ref[5:6], vec_ref[6:7]

    r = ps[:, 0:c]
    k = ps[:, c:2 * c]
    v = ps[:, 2 * c:3 * c]
    wa = ps[:, 3 * c:3 * c + 128]
    xg = ps[:, 3 * c + 128:3 * c + 384]

    w_log = -_softplus(-(w0 + _bdot(jnp.tanh(wa), w2_ref[...]))) - 0.5
    logw = -jnp.exp(w_log)
    a_sig = _sigmoid(a0 + _bdot(wa, a2_ref[...]))
    g = _bdot(_sigmoid(xg), g2_ref[...])

    same, ri, ci = _head_masks(c, n, c, n)
    tr, tc = ri % n, ci % n
    same_f = jnp.where(same, 1.0, 0.0)
    ones_bd = same_f.astype(BF16)
    strict_bd = jnp.where(same & (tr > tc), 1.0, 0.0).astype(BF16)
    incl_bd = jnp.where(same & (tr >= tc), 1.0, 0.0).astype(BF16)
    eye = jnp.where(same & (tr == tc), 1.0, 0.0)
    tril = incl_bd[0:n, 0:n]
    sc_mask = jnp.concatenate([jnp.concatenate([strict_bd, strict_bd], axis=1),
                               jnp.concatenate([incl_bd, incl_bd], axis=1)], axis=0)

    kk = k * k_k
    kk = kk / jnp.maximum(jnp.sqrt(_dot2_right(kk * kk, ones_bd)), 1e-12)
    k_mod = k * (1.0 + (a_sig - 1.0) * k_a)
    a_vec = -kk
    b_vec = kk * a_sig

    mm = lambda x, y: jnp.dot(x, y, preferred_element_type=F32)
    rep = lambda x: _stack4(x.astype(BF16))
    bd = lambda x: rep(x) * ones_bd

    ys = [[] for _ in seqs]
    for ch in range(tb // n):
        sls = [slice(b * tb + ch * n, b * tb + (ch + 1) * n) for b in seqs]
        lw = [logw[s] for s in sls]
        cum = [_dot3_left(tril, x) for x in lw]
        last = [x[n - 1:n, :] for x in cum]
        g_inv = [jnp.exp(-x) for x in cum]
        g_end = [jnp.exp(l - x) for l, x in zip(last, cum)]
        at = [a_vec[s] * jnp.exp(x - w) for s, x, w in zip(sls, cum, lw)]
        rt = [r[s] * jnp.exp(x) for s, x in zip(sls, cum)]
        bt = [b_vec[s] * gi for s, gi in zip(sls, g_inv)]
        kt = [k_mod[s] * gi for s, gi in zip(sls, g_inv)]
        bh = [b_vec[s] * ge for s, ge in zip(sls, g_end)]
        kh = [k_mod[s] * ge for s, ge in zip(sls, g_end)]

        lhs = [jnp.concatenate([bd(x), bd(y)], axis=0) for x, y in zip(at, rt)]
        rhs = [jnp.concatenate([rep(x), rep(y)], axis=0) for x, y in zip(bt, kt)]
        sc = [_bdot_nt(x, y).astype(BF16) * sc_mask for x, y in zip(lhs, rhs)]
        m_ab = [x[0:c, 0:c] for x in sc]
        m_ak = [x[0:c, c:2 * c] for x in sc]
        m_rb = [x[c:2 * c, 0:c] for x in sc]
        m_rk = [x[c:2 * c, c:2 * c] for x in sc]

        t_inv = [eye + x.astype(F32) for x in m_ab]
        m_pow = m_ab
        for _ in range(5):
            m_pow = [mm(x, x).astype(BF16) for x in m_pow]
            t_inv = [t + mm(x, t.astype(BF16)) for x, t in zip(m_pow, t_inv)]
        t_inv = [t.astype(BF16) for t in t_inv]

        bdv = [bd(v[s]) for s in sls]
        pk = [mm(x, y) for x, y in zip(m_ak, bdv)]
        yk = [mm(x, y) for x, y in zip(m_rk, bdv)]
        st = [st_ref[b] for b in seqs]
        ws = [_bdot_nt(x, s) for x, s in zip(lhs, st)]
        u_bd = [mm(t, (w[0:c] + p_).astype(BF16)) for t, w, p_ in zip(t_inv, ws, pk)]
        y_bd = [w[c:2 * c] + mm(x, u.astype(BF16)) + z for w, x, u, z in zip(ws, m_rb, u_bd, yk)]
        upd = [_bdot_tn(jnp.concatenate([_fold4(u, n), v[s]], axis=0), jnp.concatenate([x, y], axis=0))
               for u, s, x, y in zip(u_bd, sls, bh, kh)]
        for b in seqs:
            ys[b].append(_fold4(y_bd[b], n))
            st_ref[b] = st[b] * jnp.exp(last[b]) + upd[b] * same_f

    y = jnp.concatenate([blk for b in seqs for blk in ys[b]], axis=0)
    inv_n = 1.0 / HEAD_DIM
    mu_y = _dot2_right(y, ones_bd) * inv_n
    dy = y - mu_y
    var_y = _dot2_right(dy * dy, ones_bd) * inv_n
    yn = dy * lax.rsqrt(var_y + RW_LN_EPS) * ln_g + ln_b
    bonus = _dot2_right(r * k_mod * r_k, ones_bd) * v
    out = (yn + bonus) * g
    for b in seqs:
        o_ref[b] = out[b * tb:(b + 1) * tb]


def _rwkv(p_rw, mu, vec, w2, a2, g2, l, nb, tb):
    bsz, t, _ = p_rw.shape
    return pl.pallas_call(
        functools.partial(_rwkv_kernel, nb=nb, tb=tb),
        grid=(bsz // nb, t // tb),
        in_specs=[pl.BlockSpec((nb, tb, RW_PAD), lambda b, i: (b, i, 0)),
                  _layer_spec((1, RW_PAD), l), _layer_spec((8, WIDTH), l),
                  _layer_spec((128, WIDTH), l), _layer_spec((128, WIDTH), l),
                  _layer_spec((WIDTH, WIDTH), l)],
        out_specs=pl.BlockSpec((nb, tb, WIDTH), lambda b, i: (b, i, 0)),
        out_shape=jax.ShapeDtypeStruct((bsz, t, WIDTH), F32),
        scratch_shapes=[pltpu.VMEM((nb, WIDTH, WIDTH), F32), pltpu.VMEM((nb, 1, RW_PAD), F32)],
        compiler_params=_params(("parallel", "arbitrary")),
        name="rwkv7",
    )(p_rw, mu, vec, w2, a2, g2)


def _conv_kernel(p_ref, w_ref, vec_ref, o_ref, buf_ref, sh_ref, *, tb):
    @pl.when(pl.program_id(1) == 0)
    def _():
        buf_ref[0:CONV_HALO, :] = jnp.zeros((CONV_HALO, WIDTH), F32)

    p = p_ref[0]
    buf_ref[CONV_HALO:CONV_HALO + tb, :] = p[:, 0:WIDTH] * _sigmoid(p[:, WIDTH:2 * WIDTH])
    acc = jnp.zeros((tb, WIDTH), F32) + vec_ref[0:1]
    first = CONV_HALO - (CONV_WIDTH - 1)
    for s in range(8):
        last_a = (CONV_WIDTH - 1 - s) // 8
        span = tb + 8 * last_a
        sh_ref[0:span, :] = buf_ref[first + s:first + s + span, :]
        for a in range(last_a + 1):
            kx = 8 * a + s
            acc = acc + sh_ref[8 * a:8 * a + tb, :] * w_ref[kx:kx + 1, :]
    buf_ref[0:CONV_HALO, :] = buf_ref[tb:tb + CONV_HALO, :]
    u = _layer_norm(acc, vec_ref[1:2], vec_ref[2:3])
    o_ref[0] = u * _sigmoid(u)


def _conv(p_cv, w, vec, l, tb):
    bsz, t, _ = p_cv.shape
    return pl.pallas_call(
        functools.partial(_conv_kernel, tb=tb),
        grid=(bsz, t // tb),
        in_specs=[pl.BlockSpec((1, tb, CV_COLS), lambda b, i: (b, i, 0)),
                  _layer_spec((CONV_HALO, WIDTH), l), _layer_spec((8, WIDTH), l)],
        out_specs=pl.BlockSpec((1, tb, WIDTH), lambda b, i: (b, i, 0)),
        out_shape=jax.ShapeDtypeStruct((bsz, t, WIDTH), F32),
        scratch_shapes=[pltpu.VMEM((tb + CONV_HALO, WIDTH), F32), pltpu.VMEM((tb + CONV_HALO, WIDTH), F32)],
        compiler_params=_params(("parallel", "arbitrary")),
        name="conv",
    )(p_cv, w, vec)


def _gla_kernel(p_ref, a2_ref, vec_ref, lng_ref, o_ref, st_ref, *, nb, tb):
    c = WIDTH
    n = CHUNK
    dk = HEADS * GLA_DK

    seqs = range(nb)

    @pl.when(pl.program_id(1) == 0)
    def _():
        st_ref[...] = jnp.zeros_like(st_ref)

    p = jnp.concatenate([p_ref[b] for b in seqs], axis=0)
    q = p[:, 0:dk] * (GLA_DK ** -0.5)
    k = p[:, dk:2 * dk]
    v = p[:, 2 * dk:2 * dk + c]
    gr = p[:, 2 * dk + c:2 * dk + 2 * c]
    z = p[:, 2 * dk + 2 * c:2 * dk + 2 * c + 128]
    log_a = _log_sigmoid(_bdot(z, a2_ref[...]) + vec_ref[0:1]) / GLA_TAU

    same_q, _, _ = _head_masks(c, n, dk, GLA_DK)
    same, ri, ci = _head_masks(c, n, c, n)
    same_qf = jnp.where(same_q, 1.0, 0.0)
    same_qb = same_qf.astype(BF16)
    ones_bd = jnp.where(same, 1.0, 0.0).astype(BF16)
    incl_bd = jnp.where(same & ((ri % n) >= (ci % n)), 1.0, 0.0).astype(BF16)
    tril = incl_bd[0:n, 0:n]

    mm = lambda x, y: jnp.dot(x, y, preferred_element_type=F32)
    rep = lambda x: _stack4(x.astype(BF16))

    os_ = [[] for _ in seqs]
    for ch in range(tb // n):
        sls = [slice(b * tb + ch * n, b * tb + (ch + 1) * n) for b in seqs]
        cum = [_dot3_left(tril, log_a[s]) for s in sls]
        last = [x[n - 1:n, :] for x in cum]
        q_bd = [rep(q[s] * jnp.exp(x)) * same_qb for s, x in zip(sls, cum)]
        k_inv = [rep(k[s] * jnp.exp(-x)) for s, x in zip(sls, cum)]
        k_end = [k[s] * jnp.exp(l - x) for s, l, x in zip(sls, last, cum)]
        sc = [jnp.where(incl_bd > 0, _bdot_nt(x, y).astype(BF16), 0.0) for x, y in zip(q_bd, k_inv)]
        st = [st_ref[b] for b in seqs]
        o_bd = [mm(x, rep(v[s]) * ones_bd) + _bdot_nt(y, z) for x, s, y, z in zip(sc, sls, q_bd, st)]
        upd = [_bdot_tn(v[s], x) for s, x in zip(sls, k_end)]
        for b in seqs:
            os_[b].append(_fold4(o_bd[b], n))
            st_ref[b] = st[b] * jnp.exp(last[b]) + upd[b] * same_qf

    o = jnp.concatenate([blk for b in seqs for blk in os_[b]], axis=0)
    ms = _dot2_right(o * o, ones_bd) * (1.0 / HEAD_DIM)
    o = o * lax.rsqrt(ms + GLA_EPS) * lng_ref[...]
    out = o * (gr * _sigmoid(gr))
    for b in seqs:
        o_ref[b] = out[b * tb:(b + 1) * tb]


def _gla(p_gla, a2, vec, ln_g, l, nb, tb):
    bsz, t, _ = p_gla.shape
    return pl.pallas_call(
        functools.partial(_gla_kernel, nb=nb, tb=tb),
        grid=(bsz // nb, t // tb),
        in_specs=[pl.BlockSpec((nb, tb, GLA_PAD), lambda b, i: (b, i, 0)),
                  _layer_spec((128, 128), l), _layer_spec((8, 128), l), _layer_spec((1, WIDTH), l)],
        out_specs=pl.BlockSpec((nb, tb, WIDTH), lambda b, i: (b, i, 0)),
        out_shape=jax.ShapeDtypeStruct((bsz, t, WIDTH), F32),
        scratch_shapes=[pltpu.VMEM((nb, WIDTH, HEADS * GLA_DK), F32)],
        compiler_params=_params(("parallel", "arbitrary")),
        name="gla",
    )(p_gla, a2, vec, ln_g)


def _fox_gate_kernel(z_ref, bf_ref, c_ref):
    x = _log_sigmoid(z_ref[...] + bf_ref[...])
    t = x.shape[1]
    lane = lax.broadcasted_iota(jnp.int32, x.shape, 1)
    sh = 1
    while sh < t:
        x = x + jnp.where(lane >= sh, pltpu.roll(x, sh, axis=1), 0.0)
        sh *= 2
    c_ref[...] = x


def _fox_gate(z_rows, bf, l, t):
    hp, m = z_rows.shape
    return pl.pallas_call(
        _fox_gate_kernel,
        grid=(m // t,),
        in_specs=[pl.BlockSpec((hp, t), lambda b: (0, b)), _layer_spec((hp, 1), l)],
        out_specs=pl.BlockSpec((hp, t), lambda b: (0, b)),
        out_shape=jax.ShapeDtypeStruct((hp, m), F32),
        compiler_params=_params(("parallel",)),
        name="fox_gate",
    )(z_rows, bf)


def _fox_kernel(qt_ref, k_ref, vt_ref, crow_ref, ccol_ref, o_ref, s_ref, *, tq):
    c = WIDTH
    i = pl.program_id(1)
    qt = qt_ref[...] * (HEAD_DIM ** -0.5 * LOG2E)
    row_head = lax.broadcasted_iota(jnp.int32, (c, 1), 0) // HEAD_DIM
    qtm = jnp.concatenate([jnp.where(row_head == h, qt, 0.0) for h in range(HEADS)], axis=1).astype(BF16)
    cq = [crow_ref[h:h + 1, :] * LOG2E for h in range(HEADS)]
    causal = (lax.broadcasted_iota(jnp.int32, (tq, tq), 0)
              <= lax.broadcasted_iota(jnp.int32, (tq, tq), 1))

    def score(ks, slot):
        kblk = k_ref[pl.ds(ks, tq), :].astype(BF16)
        s_all = jnp.dot(kblk, qtm, preferred_element_type=F32)
        for h in range(HEADS):
            ck = ccol_ref[pl.ds(ks, tq), h:h + 1] * LOG2E
            s_ref[slot, :, h * tq:(h + 1) * tq] = s_all[:, h * tq:(h + 1) * tq] - ck

    def consume(ks, slot, carry, masked):
        tk = tq
        stats = []
        for h in range(HEADS):
            m = carry[h][0]
            s = s_ref[slot, :, h * tq:(h + 1) * tq]
            if masked:
                s = jnp.where(causal, s, NEG_BIG)
            m_new = jnp.maximum(m, jnp.max(s, axis=0, keepdims=True) + cq[h])
            alpha = jnp.exp2(m - m_new)
            pr = jnp.exp2(s - (m_new - cq[h]))
            stats.append((m_new, alpha, pr.astype(BF16)))
        new = []
        for h in range(HEADS):
            m_new, alpha, pr = stats[h]
            vt_h = vt_ref[h * HEAD_DIM:(h + 1) * HEAD_DIM, pl.ds(ks, tk)].astype(BF16)
            vt_aug = jnp.concatenate([vt_h, jnp.ones((DEN_ROWS, tk), BF16)], axis=0)
            acc = alpha * carry[h][1] + jnp.dot(vt_aug, pr, preferred_element_type=F32)
            new.append((m_new, acc))
        return tuple(new)

    init = tuple((jnp.full((1, tq), NEG_BIG, F32), jnp.zeros((HEAD_DIM + DEN_ROWS, tq), F32))
                 for _ in range(HEADS))
    score(0, 0)

    def pair(j, cr):
        ks = pl.multiple_of(j * 2 * tq, 2 * tq)
        score(ks + tq, 1)
        cr = consume(ks, 0, cr, False)
        score(ks + 2 * tq, 0)
        return consume(ks + tq, 1, cr, False)

    pairs = i // 2
    carry = lax.fori_loop(0, pairs, pair, init)
    base = pl.multiple_of(pairs * 2 * tq, 2 * tq)

    def odd_tail(cr):
        score(base + tq, 1)
        return consume(base + tq, 1, consume(base, 0, cr, False), True)

    carry = lax.cond(i % 2 == 1, odd_tail, lambda cr: consume(base, 0, cr, True), carry)
    out_t = jnp.concatenate([acc[0:HEAD_DIM] / acc[HEAD_DIM:HEAD_DIM + 1] for _, acc in carry], axis=0)
    o_ref[...] = out_t.T


def _fox(q_t, k, v_t, c_row, c_col, t, tq):
    m = k.shape[0]
    nq = t // tq
    hp = c_row.shape[0]
    return pl.pallas_call(
        functools.partial(_fox_kernel, tq=tq),
        grid=(m // t, nq),
        in_specs=[pl.BlockSpec((WIDTH, tq), lambda b, i: (0, b * nq + i)),
                  pl.BlockSpec((t, WIDTH), lambda b, i: (b, 0)),
                  pl.BlockSpec((WIDTH, t), lambda b, i: (0, b)),
                  pl.BlockSpec((hp, tq), lambda b, i: (0, b * nq + i)),
                  pl.BlockSpec((t, hp), lambda b, i: (b, 0))],
        out_specs=pl.BlockSpec((tq, WIDTH), lambda b, i: (b * nq + i, 0)),
        out_shape=jax.ShapeDtypeStruct((m, WIDTH), F32),
        scratch_shapes=[pltpu.VMEM((2, tq, HEADS * tq), F32)],
        compiler_params=_params(("parallel", "arbitrary")),
        name="fox",
    )(q_t, k, v_t, c_row, c_col)


def _merge_kernel(x_ref, y0_ref, y1_ref, y2_ref, y3_ref, wg_ref, gb_ref, u0_ref, u1_ref, u2_ref, u3_ref,
                  wo_ref, ln_ref, o_ref, *, alpha):
    x = x_ref[...]
    xb = x.astype(BF16)
    merged = None
    branches = ((y0_ref, u0_ref), (y1_ref, u1_ref), (y2_ref, u2_ref), (y3_ref, u3_ref))
    for j, (y_ref, up_ref) in enumerate(branches):
        gate = _sigmoid(jnp.dot(xb, wg_ref[:, j * D_MODEL:(j + 1) * D_MODEL],
                                preferred_element_type=F32) + gb_ref[j:j + 1, :])
        term = gate * _bdot(y_ref[...], up_ref[...])
        merged = term if merged is None else merged + term
    z = alpha * x + _bdot(merged, wo_ref[...])
    o_ref[...] = _layer_norm(z, ln_ref[0:1], ln_ref[1:2])


def _merge(x2, ys, wg, gb, ups, wo, ln, l, alpha, tm):
    m = x2.shape[0]
    row = lambda n: pl.BlockSpec((tm, n), lambda i: (i, 0))
    return pl.pallas_call(
        functools.partial(_merge_kernel, alpha=alpha),
        grid=(m // tm,),
        in_specs=[row(D_MODEL)] + [row(WIDTH)] * 4 + [
            _layer_spec((D_MODEL, GATE_COLS), l), _layer_spec((4, D_MODEL), l)]
        + [_layer_spec((WIDTH, D_MODEL), l)] * 4 + [
            _layer_spec((D_MODEL, D_MODEL), l), _layer_spec((2, D_MODEL), l, 0)],
        out_specs=row(D_MODEL),
        out_shape=jax.ShapeDtypeStruct((m, D_MODEL), F32),
        compiler_params=_params(("parallel",)),
        name="merge",
    )(x2, *ys, wg, gb, *ups, wo, ln)


def _xattn_kernel(x_ref, k_ref, v_ref, wq_ref, wo_ref, ln_ref, o_ref, *, alpha):
    x = x_ref[0]
    q = _bdot(x, wq_ref[...])
    hd = D_MODEL // HEADS
    sls = [slice(h * hd, (h + 1) * hd) for h in range(HEADS)]
    ss = [_bdot_nt(q[:, sl], k_ref[0, :, sl]) * (hd ** -0.5) for sl in sls]
    es = [jnp.exp(s - jnp.max(s, axis=-1, keepdims=True)) for s in ss]
    prs = [e / jnp.sum(e, axis=-1, keepdims=True) for e in es]
    outs = [_bdot(pr, v_ref[0, :, sl]) for pr, sl in zip(prs, sls)]
    y = _bdot(jnp.concatenate(outs, axis=1), wo_ref[...])
    o_ref[0] = _layer_norm(alpha * x + y, ln_ref[0:1], ln_ref[1:2])


def _xattn(x, k, v, wq, wo, ln, l, alpha, tm):
    bsz, t, _ = x.shape
    mlen = k.shape[2]
    return pl.pallas_call(
        functools.partial(_xattn_kernel, alpha=alpha),
        grid=(bsz, t // tm),
        in_specs=[pl.BlockSpec((1, tm, D_MODEL), lambda b, i: (b, i, 0)),
                  pl.BlockSpec((None, 1, mlen, D_MODEL), lambda b, i: (l, b, 0, 0)),
                  pl.BlockSpec((None, 1, mlen, D_MODEL), lambda b, i: (l, b, 0, 0)),
                  _layer_spec((D_MODEL, D_MODEL), l), _layer_spec((D_MODEL, D_MODEL), l),
                  _layer_spec((2, D_MODEL), l, 1)],
        out_specs=pl.BlockSpec((1, tm, D_MODEL), lambda b, i: (b, i, 0)),
        out_shape=jax.ShapeDtypeStruct((bsz, t, D_MODEL), F32),
        compiler_params=_params(("parallel", "parallel")),
        name="xattn",
    )(x, k, v, wq, wo, ln)


def _ffn_kernel(x_ref, w1_ref, w3_ref, w2_ref, ln_ref, o_ref, *, alpha):
    x = x_ref[...]
    xb = x.astype(BF16)
    y = None
    for c0, cw in FF_CHUNKS:
        h1 = jnp.dot(xb, w1_ref[:, c0:c0 + cw], preferred_element_type=F32)
        h3 = jnp.dot(xb, w3_ref[:, c0:c0 + cw], preferred_element_type=F32)
        part = _bdot(h1 * _sigmoid(h1) * h3, w2_ref[c0:c0 + cw, :])
        y = part if y is None else y + part
    o_ref[...] = _layer_norm(alpha * x + y, ln_ref[0:1], ln_ref[1:2])


def _ffn(x2, w1, w3, w2, ln, l, alpha, tm):
    m = x2.shape[0]
    return pl.pallas_call(
        functools.partial(_ffn_kernel, alpha=alpha),
        grid=(m // tm,),
        in_specs=[pl.BlockSpec((tm, D_MODEL), lambda i: (i, 0)),
                  _layer_spec((D_MODEL, D_FF), l), _layer_spec((D_MODEL, D_FF), l),
                  _layer_spec((D_FF, D_MODEL), l), _layer_spec((2, D_MODEL), l, 2)],
        out_specs=pl.BlockSpec((tm, D_MODEL), lambda i: (i, 0)),
        out_shape=jax.ShapeDtypeStruct((m, D_MODEL), F32),
        compiler_params=_params(("parallel",)),
        name="ffn",
    )(x2, w1, w3, w2, ln)


def _pad_axis(a, axis, n, before=0):
    widths = [(0, 0)] * a.ndim
    widths[axis] = (before, n - a.shape[axis] - before)
    return jnp.pad(a, widths)


def _rows8(*vs):
    depth = vs[0].shape[0]
    return _pad_axis(jnp.stack([v.reshape(depth, -1) for v in vs], axis=1), 1, 8)


def kernel(x, mem, w_in, rw_mu, rw_w0, rw_w2, rw_a0, rw_a2, rw_g2, rw_kk, rw_ka, rw_rk, rw_ln_g, rw_ln_b, rw_up, cv_w, cv_b, cv_ln_g, cv_ln_b, cv_up, gla_a2, gla_ab, gla_ln_g, gla_up, fox_bf, fox_up, gate_b, w_out, xa_wq, xa_wk, xa_wv, xa_wo, ffn_w1, ffn_w3, ffn_w2, ln_g, ln_b):
    bsz, t, d = x.shape
    depth = w_in.shape[0]
    alpha = (2.0 * depth) ** 0.25
    m = bsz * t
    mlen = mem.shape[1]
    tm = min(512, t)
    tb = min(256, t)
    nb = max(g for g in (8, 4, 2, 1) if bsz % g == 0)
    mem2 = mem.reshape(bsz * mlen, d)

    rw_vec = _rows8(rw_w0, rw_a0, rw_kk, rw_ka, rw_rk, rw_ln_g, rw_ln_b)
    rw_w2p = _pad_axis(rw_w2, 1, 128).astype(BF16)
    rw_a2p = _pad_axis(rw_a2, 1, 128, before=128 - rw_a2.shape[1]).astype(BF16)
    rw_g2p = _pad_axis(rw_g2, 1, WIDTH).astype(BF16)
    mu = _pad_axis(rw_mu, 1, RW_PAD)[:, None, :]
    cv_vec = _rows8(cv_b, cv_ln_g, cv_ln_b)
    cv_wp = _pad_axis(cv_w, 1, CONV_HALO)
    gla_a2p = _pad_axis(gla_a2, 1, 128).astype(BF16)
    gla_vec = _rows8(gla_ab)
    gla_lng = gla_ln_g[:, None, :]
    bf = _pad_axis(fox_bf, 1, 8)[:, :, None]
    lns = jnp.stack([ln_g, ln_b], axis=2)

    o_cv = RW_COLS
    o_gla = o_cv + CV_COLS
    o_fq = o_gla + GLA_COLS
    o_fk, o_fv, o_fz = o_fq + WIDTH, o_fq + 2 * WIDTH, o_fq + 3 * WIDTH
    o_gate = o_fq + FOX_COLS
    w_t = jnp.transpose(w_in, (2, 0, 1))
    w_small = _stage_w_in(w_t, ((0, RW_COLS), (o_cv, CV_COLS), (o_gla, GLA_COLS), (o_fk, WIDTH)), True)
    w_gate = _stage_w_in(w_t, ((o_gate, GATE_COLS),), True)
    w_fox_t = _stage_w_in(w_t, ((o_fq, WIDTH), (o_fv, WIDTH), (o_fz, HEADS)), False)
    ups = [_to_bf16(w) for w in (rw_up, cv_up, gla_up, fox_up)]
    wo_b, xq_b, xk_b, xv_b, xo_b = (_to_bf16(w) for w in (w_out, xa_wq, xa_wk, xa_wv, xa_wo))
    w1_b, w3_b, w2_b = (_to_bf16(w) for w in (ffn_w1, ffn_w3, ffn_w2))
    k_mem = _matmul_layers(mem2, xk_b, BF16).reshape(depth, bsz, mlen, d)
    v_mem = _matmul_layers(mem2, xv_b, BF16).reshape(depth, bsz, mlen, d)

    for l in range(depth):
        x2 = x.reshape(m, d)
        p_rw, p_cv, p_gla, f_k, f_qt, f_vt, f_zt = _in_proj(x2, w_small, w_fox_t, l, tm)
        p_rw = p_rw.reshape(bsz, t, RW_PAD)
        p_cv = p_cv.reshape(bsz, t, CV_COLS)
        p_gla = p_gla.reshape(bsz, t, GLA_PAD)

        y_rw = _rwkv(p_rw, mu, rw_vec, rw_w2p, rw_a2p, rw_g2p, l, nb, min(2 * CHUNK, t))
        y_cv = _conv(p_cv, cv_wp, cv_vec, l, tb)
        y_gla = _gla(p_gla, gla_a2p, gla_vec, gla_lng, l, nb, CHUNK)
        c_row = _fox_gate(f_zt, bf, l, t)
        y_fox = _fox(f_qt, f_k, f_vt, c_row, c_row.T, t, tb)

        ys = [y.reshape(m, WIDTH) for y in (y_rw, y_cv, y_gla)] + [y_fox]
        x2 = _merge(x2, ys, w_gate, gate_b, ups, wo_b, lns, l, alpha, tm)

        x3 = _xattn(x2.reshape(bsz, t, d), k_mem, v_mem, xq_b, xo_b, lns, l, alpha, tm)

        x2 = _ffn(x3.reshape(m, d), w1_b, w3_b, w2_b, lns, l, alpha, tm)
        x = x2.reshape(bsz, t, d)
    return x
```
